```python
import math
import jax, jax.numpy as jnp
from jax import lax
import numpy as np

D_MODEL = 1024
BATCH = 8
SEQ = 4096
DEPTH = 2
DEC_BATCH = 32
DEC_SEQ = 4
PAST_LEN = 16384
PAGE_SIZE = 128

GLA_HEADS = 4
GLA_DK = D_MODEL // 16
GLA_DV = D_MODEL // 8
GLA_RANK = 16
GLA_TAU = 16.0
GLA_CHUNK = 64
MOBA_HEADS = 4
MOBA_HD = D_MODEL // 8
MOBA_BLOCK = 256
MOBA_TOPK = 3
MOBA_QCHUNK = 16
ROPE_THETA = 500000.0
ROPE_DIM = MOBA_HD // 4
SB_HEADS = 4
SB_HD = D_MODEL // 8
SB_QBLOCK = 128
N_BRANCH = 3
FFN_DIM = ((8 * D_MODEL // 3 + 127) // 128) * 128
N_MOD = 9
EPS = 1e-6

IN_SPLIT_SIZES = (GLA_HEADS * GLA_DK, GLA_HEADS * GLA_DK, GLA_HEADS * GLA_DV, GLA_HEADS * GLA_DV, GLA_RANK,
                  MOBA_HEADS * MOBA_HD, MOBA_HEADS * MOBA_HD, MOBA_HEADS * MOBA_HD,
                  SB_HEADS * SB_HD, SB_HEADS * SB_HD, SB_HEADS * SB_HD,
                  N_BRANCH * D_MODEL)
IN_WIDTH = sum(IN_SPLIT_SIZES)
IN_SPLIT_IDX = tuple(sum(IN_SPLIT_SIZES[:i + 1]) for i in range(len(IN_SPLIT_SIZES) - 1))

kernel_name = "hybrid_gla_moba_stickbreak_decode_step"


def rmsnorm(x, g):
    xf = x.astype(jnp.float32)
    y = xf * lax.rsqrt(jnp.mean(xf * xf, axis=-1, keepdims=True) + EPS)
    return (y * g.astype(jnp.float32)).astype(x.dtype)


def swiglu(h, w_in, w_down):
    gate, up = jnp.split(h @ w_in, 2, axis=-1)
    return (jax.nn.silu(gate) * up) @ w_down


def split_heads(a, n_heads):
    return a.reshape(a.shape[0], a.shape[1], n_heads, -1)


def partial_rope(x, pos):
    half = ROPE_DIM // 2
    inv = ROPE_THETA ** (-jnp.arange(half, dtype=jnp.float32) / half)
    ang = pos.astype(jnp.float32)[:, None] * inv[None, :]
    cos = jnp.cos(ang)[None, :, None, :]
    sin = jnp.sin(ang)[None, :, None, :]
    xr = x[..., :ROPE_DIM].astype(jnp.float32)
    x1, x2 = xr[..., :half], xr[..., half:]
    rot = jnp.concatenate([x1 * cos - x2 * sin, x2 * cos + x1 * sin], axis=-1).astype(x.dtype)
    return jnp.concatenate([rot, x[..., ROPE_DIM:]], axis=-1)


def gla_chunked(q, k, v, log_a, s0):
    B, T, H, DK = q.shape
    DV = v.shape[-1]
    C = math.gcd(T, GLA_CHUNK)
    n = T // C

    def to_chunks(a):
        return a.astype(jnp.float32).reshape(B, n, C, H, a.shape[-1]).transpose(1, 0, 2, 3, 4)

    causal = jnp.tril(jnp.ones((C, C), dtype=bool))

    def step(S, inp):
        qc, kc, vc, gc = inp
        b = jnp.cumsum(gc, axis=1)
        o_inter = jnp.einsum('bthk,bhkv->bthv', qc * jnp.exp(b), S)
        diff = b[:, :, None] - b[:, None, :]
        decay = jnp.exp(jnp.where(causal[None, :, :, None, None], diff, -jnp.inf))
        att = jnp.einsum('bthk,bshk,btshk->bths', qc, kc, decay)
        o = o_inter + jnp.einsum('bths,bshv->bthv', att, vc)
        b_last = b[:, -1]
        S = jnp.exp(b_last)[..., None] * S + jnp.einsum(
            'bshk,bshv->bhkv', kc * jnp.exp(b_last[:, None] - b), vc)
        return S, o

    S, o = lax.scan(step, s0.astype(jnp.float32),
                    (to_chunks(q), to_chunks(k), to_chunks(v), to_chunks(log_a)))
    o = o.transpose(1, 0, 2, 3, 4).reshape(B, T, H, DV)
    return o, S


def moba_attend(q, k, v, q_pos):
    B, T, H, D = q.shape
    L = k.shape[1]
    nb = -(-L // MOBA_BLOCK)
    pad = nb * MOBA_BLOCK - L
    padw = ((0, 0), (0, pad), (0, 0), (0, 0))
    kb = jnp.pad(k, padw).reshape(B, nb, MOBA_BLOCK, H, D).transpose(0, 3, 1, 2, 4)
    vb = jnp.pad(v, padw).reshape(B, nb, MOBA_BLOCK, H, D).transpose(0, 3, 1, 2, 4)
    kmean = jnp.mean(kb, axis=3, dtype=jnp.float32)
    ksel = min(MOBA_TOPK, nb)
    qn = math.gcd(T, MOBA_QCHUNK)
    n = T // qn
    bi = jnp.arange(B)[:, None, None]
    hi = jnp.arange(H)[None, None, :]
    offs = jnp.arange(MOBA_BLOCK)
    blk_ids = jnp.arange(nb)
    scale = D ** -0.5

    def chunk(inp):
        qc, pc = inp
        Q = qc.shape[1]
        own = pc // MOBA_BLOCK
        gate = jnp.einsum('bqhd,bhnd->bqhn', qc.astype(jnp.float32), kmean)
        past_ok = blk_ids[None, :] < own[:, None]
        gate = jnp.where(past_ok[None, :, None, :], gate, -jnp.inf)
        _, idx = lax.top_k(gate, ksel)
        sel_ok = idx < own[None, :, None, None]
        k_sel = kb[bi[..., None], hi[..., None], idx]
        v_sel = vb[bi[..., None], hi[..., None], idx]
        own_i = own[None, :, None]
        k_own = kb[bi, hi, own_i]
        v_own = vb[bi, hi, own_i]
        s_sel = jnp.einsum('bqhd,bqhnjd->bqhnj', qc, k_sel).astype(jnp.float32) * scale
        s_sel = jnp.where(sel_ok[..., None], s_sel, -jnp.inf).reshape(B, Q, H, ksel * MOBA_BLOCK)
        s_own = jnp.einsum('bqhd,bqhjd->bqhj', qc, k_own).astype(jnp.float32) * scale
        own_ok = (own[:, None] * MOBA_BLOCK + offs[None, :]) <= pc[:, None]
        s_own = jnp.where(own_ok[None, :, None, :], s_own, -jnp.inf)
        p = jax.nn.softmax(jnp.concatenate([s_sel, s_own], axis=-1), axis=-1).astype(v.dtype)
        p_sel = p[..., :ksel * MOBA_BLOCK].reshape(B, Q, H, ksel, MOBA_BLOCK)
        p_own = p[..., ksel * MOBA_BLOCK:]
        return (jnp.einsum('bqhnj,bqhnjd->bqhd', p_sel, v_sel)
                + jnp.einsum('bqhj,bqhjd->bqhd', p_own, v_own))

    qs = q.reshape(B, n, qn, H, D).transpose(1, 0, 2, 3, 4)
    ps = q_pos.reshape(n, qn)
    o = lax.map(chunk, (qs, ps))
    return o.transpose(1, 0, 2, 3, 4).reshape(B, T, H, D)


def stick_breaking(q, k, v, q_pos):
    B, T, H, D = q.shape
    L = k.shape[1]
    qb = math.gcd(T, SB_QBLOCK)
    n = T // qb
    kpos = jnp.arange(L)
    scale = D ** -0.5

    def block(inp):
        qc, pc = inp
        z = jnp.einsum('bqhd,blhd->bhql', qc, k).astype(jnp.float32) * scale
        mask = kpos[None, :] < pc[:, None]
        log_beta = jax.nn.log_sigmoid(z)
        log_rem = jnp.where(mask, jax.nn.log_sigmoid(-z), 0.0)
        after = lax.cumsum(log_rem, axis=3, reverse=True) - log_rem
        w = jnp.where(mask, jnp.exp(log_beta + after), 0.0).astype(v.dtype)
        return jnp.einsum('bhql,blhd->bqhd', w, v)

    qs = q.reshape(B, n, qb, H, D).transpose(1, 0, 2, 3, 4)
    ps = q_pos.reshape(n, qb)
    o = lax.map(block, (qs, ps))
    return o.transpose(1, 0, 2, 3, 4).reshape(B, T, H, D)


def mixer(h, pos, p, gla_s0, moba_past, sb_past):
    B, T, _ = h.shape
    (gq, gk, gv, gr, glr, mq, mk, mv, sq, sk, sv, gates) = jnp.split(h @ p['w_in'], IN_SPLIT_IDX, axis=-1)
    log_a = jax.nn.log_sigmoid((glr @ p['w_gla_alpha'] + p['b_gla_alpha']).astype(jnp.float32)) / GLA_TAU
    o_a, s_a = gla_chunked(split_heads(gq, GLA_HEADS) * (GLA_DK ** -0.5), split_heads(gk, GLA_HEADS),
                           split_heads(gv, GLA_HEADS), split_heads(log_a, GLA_HEADS), gla_s0)
    o_a = rmsnorm(o_a.astype(h.dtype), p['g_gla_head']) * jax.nn.silu(split_heads(gr, GLA_HEADS))
    o_a = o_a.reshape(B, T, -1)
    mq = partial_rope(split_heads(mq, MOBA_HEADS), pos)
    mk = partial_rope(split_heads(mk, MOBA_HEADS), pos)
    mv = split_heads(mv, MOBA_HEADS)
    if moba_past is None:
        mk_all, mv_all = mk, mv
    else:
        mk_all = jnp.concatenate([moba_past[0], mk], axis=1)
        mv_all = jnp.concatenate([moba_past[1], mv], axis=1)
    o_b = moba_attend(mq, mk_all, mv_all, pos).reshape(B, T, -1)
    sq = split_heads(sq, SB_HEADS)
    sk = split_heads(sk, SB_HEADS)
    sv = split_heads(sv, SB_HEADS)
    if sb_past is None:
        sk_all, sv_all = sk, sv
    else:
        sk_all = jnp.concatenate([sb_past[0], sk], axis=1)
        sv_all = jnp.concatenate([sb_past[1], sv], axis=1)
    o_c = stick_breaking(sq, sk_all, sv_all, pos).reshape(B, T, -1)
    g = jax.nn.sigmoid(gates.astype(jnp.float32)).astype(h.dtype).reshape(B, T, N_BRANCH, D_MODEL)
    merged = (g[:, :, 0] * (o_a @ p['w_br_gla']) + g[:, :, 1] * (o_b @ p['w_br_moba'])
              + g[:, :, 2] * (o_c @ p['w_br_sb']))
    return merged @ p['w_out'], (mk, mv, sk, sv, s_a)


def layer(x, c, pos, p, gla_s0, moba_past, sb_past):
    mod = (jax.nn.silu(c) @ p['w_ada'] + p['b_ada'])[:, None, :]
    sh1, sc1, gt1, shm, scm, gtm, sh2, sc2, gt2 = jnp.split(mod, N_MOD, axis=-1)
    h = rmsnorm(x, p['g_ffn1_pre']) * (1 + sc1) + sh1
    x = x + 0.5 * gt1 * rmsnorm(swiglu(h, p['w_ffn1_in'], p['w_ffn1_down']), p['g_ffn1_post'])
    h = rmsnorm(x, p['g_mix_pre']) * (1 + scm) + shm
    m, new = mixer(h, pos, p, gla_s0, moba_past, sb_past)
    x = x + gtm * rmsnorm(m, p['g_mix_post'])
    h = rmsnorm(x, p['g_ffn2_pre']) * (1 + sc2) + sh2
    x = x + 0.5 * gt2 * rmsnorm(swiglu(h, p['w_ffn2_in'], p['w_ffn2_down']), p['g_ffn2_post'])
    return x, new


def setup_inputs(seed: int = 0) -> dict:
    key = jax.random.key(seed)
    ks = iter(jax.random.split(key, 48))
    f32 = jnp.float32

    def nrm(shape, scale):
        return jax.random.normal(next(ks), shape, f32) * scale

    def gain(shape):
        return 1.0 + nrm(shape, 0.05)

    n_pages = PAST_LEN // PAGE_SIZE
    n_used = DEC_BATCH * n_pages
    n_phys = n_used + (n_used + 3) // 4
    perm = jax.random.permutation(next(ks), n_phys)
    page_table = perm[:n_used].reshape(DEC_BATCH, n_pages).astype(jnp.int32)
    D = D_MODEL
    return {
        "x_prompt": nrm((BATCH, SEQ, D), 1.0),
        "x_sample": nrm((DEC_BATCH, DEC_SEQ, D), 1.0),
        "cache_moba_k": nrm((DEPTH, n_phys, PAGE_SIZE, MOBA_HEADS, MOBA_HD), 1.0),
        "cache_moba_v": nrm((DEPTH, n_phys, PAGE_SIZE, MOBA_HEADS, MOBA_HD), 1.0),
        "cache_sb_k": nrm((DEPTH, n_phys, PAGE_SIZE, SB_HEADS, SB_HD), 1.0),
        "cache_sb_v": nrm((DEPTH, n_phys, PAGE_SIZE, SB_HEADS, SB_HD), 1.0),
        "state_gla": nrm((DEPTH, DEC_BATCH, GLA_HEADS, GLA_DK, GLA_DV), 1.0),
        "page_table": page_table,
        "c_prompt": nrm((BATCH, D), 1.0),
        "c_sample": nrm((DEC_BATCH, D), 1.0),
        "w_ada": nrm((DEPTH, D, N_MOD * D), 0.5 * D ** -0.5),
        "b_ada": nrm((DEPTH, N_MOD * D), 0.01),
        "g_ffn1_pre": gain((DEPTH, D)),
        "g_ffn1_post": gain((DEPTH, D)),
        "g_mix_pre": gain((DEPTH, D)),
        "g_mix_post": gain((DEPTH, D)),
        "g_ffn2_pre": gain((DEPTH, D)),
        "g_ffn2_post": gain((DEPTH, D)),
        "w_ffn1_in": nrm((DEPTH, D, 2 * FFN_DIM), D ** -0.5),
        "w_ffn1_down": nrm((DEPTH, FFN_DIM, D), FFN_DIM ** -0.5),
        "w_ffn2_in": nrm((DEPTH, D, 2 * FFN_DIM), D ** -0.5),
        "w_ffn2_down": nrm((DEPTH, FFN_DIM, D), FFN_DIM ** -0.5),
        "w_in": nrm((DEPTH, D, IN_WIDTH), D ** -0.5),
        "w_gla_alpha": nrm((DEPTH, GLA_RANK, GLA_HEADS * GLA_DK), GLA_RANK ** -0.5),
        "b_gla_alpha": nrm((DEPTH, GLA_HEADS * GLA_DK), 0.01),
        "g_gla_head": gain((DEPTH, GLA_DV)),
        "w_br_gla": nrm((DEPTH, GLA_HEADS * GLA_DV, D), (GLA_HEADS * GLA_DV) ** -0.5),
        "w_br_moba": nrm((DEPTH, MOBA_HEADS * MOBA_HD, D), (MOBA_HEADS * MOBA_HD) ** -0.5),
        "w_br_sb": nrm((DEPTH, SB_HEADS * SB_HD, D), (SB_HEADS * SB_HD) ** -0.5),
        "w_out": nrm((DEPTH, D, D), D ** -0.5),
    }


def reference(x_prompt, x_sample, cache_moba_k, cache_moba_v, cache_sb_k, cache_sb_v, state_gla, page_table,
              c_prompt, c_sample, w_ada, b_ada, g_ffn1_pre, g_ffn1_post, g_mix_pre, g_mix_post, g_ffn2_pre,
              g_ffn2_post, w_ffn1_in, w_ffn1_down, w_ffn2_in, w_ffn2_down, w_in, w_gla_alpha, b_gla_alpha,
              g_gla_head, w_br_gla, w_br_moba, w_br_sb, w_out):
    B, T, _ = x_prompt.shape
    DB, TS, _ = x_sample.shape
    page = cache_moba_k.shape[2]
    past_len = page_table.shape[1] * page
    pos_p = jnp.arange(T, dtype=jnp.int32)
    pos_s = past_len + jnp.arange(TS, dtype=jnp.int32)

    def gather(pool):
        g = pool[page_table]
        return g.reshape(DB, past_len, g.shape[-2], g.shape[-1])

    yp, ys = x_prompt, x_sample
    p_mk, p_mv, p_sk, p_sv, p_st = [], [], [], [], []
    s_mk, s_mv, s_sk, s_sv, s_st = [], [], [], [], []
    for l in range(DEPTH):
        p = dict(w_ada=w_ada[l], b_ada=b_ada[l], g_ffn1_pre=g_ffn1_pre[l], g_ffn1_post=g_ffn1_post[l],
                 g_mix_pre=g_mix_pre[l], g_mix_post=g_mix_post[l], g_ffn2_pre=g_ffn2_pre[l],
                 g_ffn2_post=g_ffn2_post[l], w_ffn1_in=w_ffn1_in[l], w_ffn1_down=w_ffn1_down[l],
                 w_ffn2_in=w_ffn2_in[l], w_ffn2_down=w_ffn2_down[l], w_in=w_in[l],
                 w_gla_alpha=w_gla_alpha[l], b_gla_alpha=b_gla_alpha[l], g_gla_head=g_gla_head[l],
                 w_br_gla=w_br_gla[l], w_br_moba=w_br_moba[l], w_br_sb=w_br_sb[l], w_out=w_out[l])
        s0 = jnp.zeros((B, GLA_HEADS, GLA_DK, GLA_DV), jnp.float32)
        yp, (mk, mv, sk, sv, st) = layer(yp, c_prompt, pos_p, p, s0, None, None)
        p_mk.append(mk); p_mv.append(mv); p_sk.append(sk); p_sv.append(sv)
        p_st.append(st.astype(state_gla.dtype))
        ys, (mk, mv, sk, sv, st) = layer(
            ys, c_sample, pos_s, p, state_gla[l],
            (gather(cache_moba_k[l]), gather(cache_moba_v[l])),
            (gather(cache_sb_k[l]), gather(cache_sb_v[l])))
        s_mk.append(mk); s_mv.append(mv); s_sk.append(sk); s_sv.append(sv)
        s_st.append(st.astype(state_gla.dtype))
    new_moba_k_prompt = jnp.stack(p_mk)
    new_moba_v_prompt = jnp.stack(p_mv)
    new_sb_k_prompt = jnp.stack(p_sk)
    new_sb_v_prompt = jnp.stack(p_sv)
    new_gla_state_prompt = jnp.stack(p_st)
    new_moba_k_sample = jnp.stack(s_mk)
    new_moba_v_sample = jnp.stack(s_mv)
    new_sb_k_sample = jnp.stack(s_sk)
    new_sb_v_sample = jnp.stack(s_sv)
    new_gla_state_sample = jnp.stack(s_st)
    return (yp, ys, new_moba_k_prompt, new_moba_v_prompt, new_sb_k_prompt, new_sb_v_prompt,
            new_gla_state_prompt, new_moba_k_sample, new_moba_v_sample, new_sb_k_sample, new_sb_v_sample,
            new_gla_state_sample)
```

```python
import functools
import math

import jax
import jax.numpy as jnp
from jax import lax
from jax.experimental import pallas as pl
from jax.experimental.pallas import tpu as pltpu

F32 = jnp.float32
BF16 = jnp.bfloat16

EPS = 1e-6
N_HEADS = 4
GLA_TAU = 16.0
GLA_CHUNK = 64
MOBA_BLOCK = 256
MOBA_TOPK = 3
ROPE_THETA = 500000.0
N_MOD = 9
LANES = 128
VMEM_LIMIT_BYTES = 56 * 1024 * 1024

NEG_INF = float("-inf")


def _dot(a, b):
    return jnp.dot(a, b, preferred_element_type=F32)


def _dot_nt(a, b):
    return lax.dot_general(a, b, (((1,), (1,)), ((), ())), preferred_element_type=F32)


def _dot_tn(a, b):
    return lax.dot_general(a, b, (((0,), (0,)), ((), ())), preferred_element_type=F32)


def _rms(x, g):
    return x * lax.rsqrt(jnp.mean(x * x, axis=-1, keepdims=True) + EPS) * g


def _silu(x):
    return x * jax.nn.sigmoid(x)


def _softplus_neg_abs(z):
    return jnp.log1p(jnp.exp(-jnp.abs(z)))


def _split_hi_lo(x):
    hi = x.astype(BF16)
    lo = (x - hi.astype(F32)).astype(BF16)
    return hi, lo


def _params(*sem):
    return pltpu.CompilerParams(dimension_semantics=sem, vmem_limit_bytes=VMEM_LIMIT_BYTES)


def _resident(shape, index_map):
    return pl.BlockSpec(shape, index_map, pipeline_mode=pl.Buffered(1))


def _mod_spec(mod, tm, rows_per_group):
    g, r, d = mod.shape
    if r == 1:
        tiles_per_group = rows_per_group // tm
        return pl.BlockSpec((None, 1, d), lambda i: (i // tiles_per_group, 0, 0))
    assert g == 1 and r == tm
    return pl.BlockSpec((None, r, d), lambda i: (0, 0, 0))


def _ada_kernel(c_ref, w_ref, b_ref, o_ref):
    a = _silu(c_ref[...]).astype(BF16)
    o_ref[...] = _dot(a, w_ref[...].astype(BF16)) + b_ref[...]


def _ada(c_all, w_ada, b_ada):
    depth, d, n = w_ada.shape
    m = c_all.shape[0]
    tn = 1024
    return pl.pallas_call(
        _ada_kernel,
        grid=(depth, n // tn),
        in_specs=[
            pl.BlockSpec((m, d), lambda l, j: (0, 0)),
            pl.BlockSpec((None, d, tn), lambda l, j: (l, 0, j)),
            pl.BlockSpec((None, 1, tn), lambda l, j: (l, 0, j)),
        ],
        out_specs=pl.BlockSpec((None, m, tn), lambda l, j: (l, 0, j)),
        out_shape=jax.ShapeDtypeStruct((depth, m, n), F32),
        compiler_params=_params("arbitrary", "arbitrary"),
        name="ada_mod",
    )(c_all, w_ada, b_ada.reshape(depth, 1, n))


def _ffn_kernel(x_ref, sh_ref, sc_ref, gt_ref, gpre_ref, gpost_ref, wg_ref, wu_ref, wd_ref,
                o_ref, a_ref, *, tf):
    x = x_ref[...]
    h = (_rms(x, gpre_ref[...]) * (1.0 + sc_ref[...]) + sh_ref[...]).astype(BF16)
    f = wd_ref.shape[0]
    for c in range(f // tf):
        sl = slice(c * tf, (c + 1) * tf)
        g = _dot(h, wg_ref[:, sl])
        u = _dot(h, wu_ref[:, sl])
        a_ref[:, sl] = (_silu(g) * u).astype(BF16)
    y = _dot(a_ref[...], wd_ref[...])
    o_ref[...] = x + 0.5 * gt_ref[...] * _rms(y, gpost_ref[...])


def _ffn(x, sh, sc, gt, g_pre, g_post, w_in, w_down, *, tm, rows_per_group):
    n, d = x.shape
    f = w_down.shape[0]
    tf = 256
    assert f % tf == 0 and n % tm == 0
    row = pl.BlockSpec((tm, d), lambda i: (i, 0))
    vec = pl.BlockSpec((1, d), lambda i: (0, 0))
    return pl.pallas_call(
        functools.partial(_ffn_kernel, tf=tf),
        grid=(n // tm,),
        in_specs=[row, _mod_spec(sh, tm, rows_per_group), _mod_spec(sc, tm, rows_per_group),
                  _mod_spec(gt, tm, rows_per_group), vec, vec,
                  _resident((d, f), lambda i: (0, 0)), _resident((d, f), lambda i: (0, 1)),
                  _resident((f, d), lambda i: (0, 0))],
        out_specs=row,
        out_shape=jax.ShapeDtypeStruct((n, d), F32),
        scratch_shapes=[pltpu.VMEM((tm, f), BF16)],
        compiler_params=_params("arbitrary"),
        name="ffn",
    )(x, sh, sc, gt, g_pre, g_post, w_in, w_in, w_down)


def _rope(x, cos_t, sin_t):
    quarter = cos_t.shape[-1] // 8
    lane = lax.broadcasted_iota(jnp.int32, cos_t.shape, 1)
    outs = []
    for h in range(x.shape[-1] // LANES):
        xh = x[:, h * LANES:(h + 1) * LANES]
        partner = jnp.where(lane < quarter, pltpu.roll(xh, LANES - quarter, 1), pltpu.roll(xh, quarter, 1))
        outs.append(xh * cos_t + partner * sin_t)
    return jnp.concatenate(outs, axis=-1)


def _mixin_kernel(x_ref, sh_ref, sc_ref, g_ref, cos_ref, sin_ref, w_ref, wa_ref, ba_ref,
                  gq_ref, gk_ref, gv_ref, la_ref,
                  mq_ref, mk_ref, mkb_ref, mv_ref, mvb_ref,
                  sq_ref, sk_ref, skb_ref, sv_ref, svb_ref, *, dk_total, dv_total, hd_total):
    x = x_ref[...]
    h = (_rms(x, g_ref[...]) * (1.0 + sc_ref[...]) + sh_ref[...]).astype(BF16)
    o = 0
    qk = _dot(h, w_ref[:, o:o + 2 * dk_total]); o += 2 * dk_total
    dk = dk_total // N_HEADS
    gq_ref[...] = qk[:, :dk_total] * (dk ** -0.5)
    gk_ref[...] = qk[:, dk_total:]
    gv_ref[...] = _dot(h, w_ref[:, o:o + dv_total]); o += dv_total
    glr = _dot(h, w_ref[:, o:o + LANES]); o += LANES
    alpha = _dot(glr.astype(BF16), wa_ref[...]) + ba_ref[...]
    la_ref[...] = (jnp.minimum(alpha, 0.0) - _softplus_neg_abs(alpha)) / GLA_TAU
    cos_t = cos_ref[...]
    sin_t = sin_ref[...]
    mq = _rope(_dot(h, w_ref[:, o:o + hd_total]), cos_t, sin_t); o += hd_total
    mq_ref[...] = mq.astype(BF16)
    mk = _rope(_dot(h, w_ref[:, o:o + hd_total]), cos_t, sin_t); o += hd_total
    mk_ref[...] = mk
    mkb_ref[...] = mk.astype(BF16)
    mv = _dot(h, w_ref[:, o:o + hd_total]); o += hd_total
    mv_ref[...] = mv
    mvb_ref[...] = mv.astype(BF16)
    sq_ref[...] = _dot(h, w_ref[:, o:o + hd_total]).astype(BF16); o += hd_total
    sk = _dot(h, w_ref[:, o:o + hd_total]); o += hd_total
    sk_ref[...] = sk
    skb_ref[...] = sk.astype(BF16)
    sv = _dot(h, w_ref[:, o:o + hd_total]); o += hd_total
    sv_ref[...] = sv
    svb_ref[...] = sv.astype(BF16)


def _mixin(x, sh, sc, g_pre, cos_t, sin_t, w1, w_alpha, b_alpha, *, tm, rows_per_group,
           dk_total, dv_total, hd_total):
    n, d = x.shape
    width = w1.shape[1]
    table_tiles = cos_t.shape[0] // tm
    row = lambda w: pl.BlockSpec((tm, w), lambda i: (i, 0))
    vec = lambda w: pl.BlockSpec((1, w), lambda i: (0, 0))
    table = pl.BlockSpec((tm, LANES), lambda i: (i % table_tiles, 0))
    widths = [(dk_total, F32), (dk_total, F32), (dv_total, F32), (dk_total, F32),
              (hd_total, BF16), (hd_total, F32), (hd_total, BF16), (hd_total, F32), (hd_total, BF16),
              (hd_total, BF16), (hd_total, F32), (hd_total, BF16), (hd_total, F32), (hd_total, BF16)]
    return pl.pallas_call(
        functools.partial(_mixin_kernel, dk_total=dk_total, dv_total=dv_total, hd_total=hd_total),
        grid=(n // tm,),
        in_specs=[row(d), _mod_spec(sh, tm, rows_per_group), _mod_spec(sc, tm, rows_per_group),
                  vec(d), table, table,
                  _resident((d, width), lambda i: (0, 0)),
                  _resident(w_alpha.shape, lambda i: (0, 0)), vec(dk_total)],
        out_specs=[row(w) for w, _ in widths],
        out_shape=[jax.ShapeDtypeStruct((n, w), t) for w, t in widths],
        compiler_params=_params("arbitrary"),
        name="mixin",
    )(x, sh, sc, g_pre, cos_t, sin_t, w1, w_alpha, b_alpha)


def _cumsum_rows(x):
    n = x.shape[0]
    row = lax.broadcasted_iota(jnp.int32, x.shape, 0)
    s = 1
    while s < n:
        x = x + jnp.where(row >= s, pltpu.roll(x, s, 0), 0.0)
        s *= 2
    return x


def _gla_kernel(q_ref, k_ref, la_ref, v_ref, s0_ref, g_ref, o_ref, sout_ref, st_ref, *, chunk):
    j = pl.program_id(1)
    n_pairs = N_HEADS // 2
    dk = s0_ref.shape[1]

    @pl.when(j == 0)
    def _():
        for p in range(n_pairs):
            pair = jnp.concatenate([s0_ref[2 * p], s0_ref[2 * p + 1]], axis=0)
            st_ref[p] = pair.T

    tt = q_ref.shape[0]
    lane = lax.broadcasted_iota(jnp.int32, (1, LANES), 1)
    first = lane < dk
    tril = (lax.broadcasted_iota(jnp.int32, (chunk, chunk), 1)
            <= lax.broadcasted_iota(jnp.int32, (chunk, chunk), 0))
    g = g_ref[...]

    def body(c, carry):
        rs = pl.ds(pl.multiple_of(c * chunk, chunk), chunk)
        b = _cumsum_rows(la_ref[rs, :])
        b_last = b[chunk - 1:chunk, :]
        q = q_ref[rs, :]
        k = k_ref[rs, :]
        qe = q * jnp.exp(b)
        ki = k * jnp.exp(-b)
        kd = k * jnp.exp(b_last - b)
        eb = jnp.exp(b_last)
        for p in range(n_pairs):
            ls = slice(p * LANES, (p + 1) * LANES)
            st = st_ref[p]
            st_b = st.astype(BF16)
            ki_b = ki[:, ls].astype(BF16)
            upd = jnp.zeros_like(st)
            for hl in range(2):
                hh = 2 * p + hl
                msk = first if hl == 0 else jnp.logical_not(first)
                qm = jnp.where(msk, qe[:, ls], 0.0).astype(BF16)
                kdm = jnp.where(msk, kd[:, ls], 0.0).astype(BF16)
                vs = slice(hh * LANES, (hh + 1) * LANES)
                vh = v_ref[rs, vs].astype(BF16)
                att = jnp.where(tril, _dot_nt(qm, ki_b), 0.0)
                o = _dot_nt(qm, st_b) + _dot(att.astype(BF16), vh)
                o_ref[rs, vs] = _rms(o, g)
                upd = upd + _dot_tn(vh, kdm)
            st_ref[p] = st * eb[:, ls] + upd
        return carry

    lax.fori_loop(0, tt // chunk, body, 0)

    @pl.when(j == pl.num_programs(1) - 1)
    def _():
        for p in range(n_pairs):
            t = st_ref[p].T
            sout_ref[2 * p] = t[:dk]
            sout_ref[2 * p + 1] = t[dk:]


def _gla(q, k, la, v, s0, g_head, *, tt, chunk):
    b, t, dkt = q.shape
    dvt = v.shape[-1]
    _, nh, dk, dv = s0.shape
    assert nh == N_HEADS and 2 * dk == LANES and dv == LANES and t % tt == 0 and tt % chunk == 0
    seq = lambda w: pl.BlockSpec((None, tt, w), lambda i, j: (i, j, 0))
    st = pl.BlockSpec((None, nh, dk, dv), lambda i, j: (i, 0, 0, 0))
    return pl.pallas_call(
        functools.partial(_gla_kernel, chunk=chunk),
        grid=(b, t // tt),
        in_specs=[seq(dkt), seq(dkt), seq(dkt), seq(dvt), st, pl.BlockSpec((1, dv), lambda i, j: (0, 0))],
        out_specs=[seq(dvt), st],
        out_shape=[jax.ShapeDtypeStruct((b, t, dvt), F32), jax.ShapeDtypeStruct(s0.shape, F32)],
        scratch_shapes=[pltpu.VMEM((nh // 2, LANES, LANES), F32)],
        compiler_params=_params("arbitrary", "arbitrary"),
        name="gla",
    )(q, k, la, v, s0, g_head)


def _kmean_kernel(k_ref, o_ref):
    o_ref[...] = jnp.sum(k_ref[...], axis=0, keepdims=True) * (1.0 / k_ref.shape[0])


def _kmean(k):
    b, t, w = k.shape
    nb = t // MOBA_BLOCK
    out = pl.pallas_call(
        _kmean_kernel,
        grid=(b, nb),
        in_specs=[pl.BlockSpec((None, MOBA_BLOCK, w), lambda i, j: (i, j, 0))],
        out_specs=pl.BlockSpec((None, None, 1, w), lambda i, j: (i, j, 0, 0)),
        out_shape=jax.ShapeDtypeStruct((b, nb, 1, w), F32),
        compiler_params=_params("arbitrary", "arbitrary"),
        name="moba_kmean",
    )(k)
    return out.reshape(b, nb, w)


def _top_blocks(gate, allowed, blk, n_blocks):
    g = jnp.where(allowed, gate, NEG_INF)
    sel = jnp.zeros(gate.shape, jnp.bool_)
    for _ in range(MOBA_TOPK):
        m = jnp.max(g, axis=-1, keepdims=True)
        is_m = jnp.logical_and(g == m, allowed)
        idx = jnp.min(jnp.where(is_m, blk, n_blocks), axis=-1, keepdims=True)
        pick = jnp.logical_and(blk == idx, is_m)
        sel = jnp.logical_or(sel, pick)
        g = jnp.where(pick, NEG_INF, g)
    return sel


def _moba_kernel(q_ref, k_ref, v_ref, km_ref, o_ref, *, scale):
    own = pl.program_id(1)
    tq = q_ref.shape[0]
    nb = km_ref.shape[0]
    hd = LANES
    blk = lax.broadcasted_iota(jnp.int32, (tq, nb), 1)
    allowed = blk < own
    causal = (lax.broadcasted_iota(jnp.int32, (tq, tq), 1)
              <= lax.broadcasted_iota(jnp.int32, (tq, tq), 0))
    own_rows = pl.ds(pl.multiple_of(own * tq, tq), tq)
    for h in range(N_HEADS):
        hs = slice(h * hd, (h + 1) * hd)
        qh = q_ref[:, hs]
        km_hi, km_lo = _split_hi_lo(km_ref[:, hs])
        gate = _dot_nt(qh, km_hi) + _dot_nt(qh, km_lo)
        sel = _top_blocks(gate, allowed, blk, nb).astype(F32)

        s = jnp.where(causal, _dot_nt(qh, k_ref[own_rows, hs]) * scale, NEG_INF)
        m = jnp.max(s, axis=-1, keepdims=True)
        p = jnp.exp(s - m)
        l = jnp.sum(p, axis=-1, keepdims=True)
        acc = _dot(p.astype(BF16), v_ref[own_rows, hs])

        def body(n, carry):
            m, l, acc = carry
            rows = pl.ds(pl.multiple_of(n * tq, tq), tq)
            chosen = jnp.sum(jnp.where(blk == n, sel, 0.0), axis=-1, keepdims=True) > 0.0
            s = jnp.where(chosen, _dot_nt(qh, k_ref[rows, hs]) * scale, NEG_INF)
            m_new = jnp.maximum(m, jnp.max(s, axis=-1, keepdims=True))
            a = jnp.exp(m - m_new)
            p = jnp.exp(s - m_new)
            l = a * l + jnp.sum(p, axis=-1, keepdims=True)
            acc = a * acc + _dot(p.astype(BF16), v_ref[rows, hs])
            return m_new, l, acc

        m, l, acc = lax.fori_loop(0, own, body, (m, l, acc))
        o_ref[:, hs] = (acc / l).astype(o_ref.dtype)


def _moba(q, k, v, kmean):
    b, t, w = q.shape
    nb = kmean.shape[1]
    tq = MOBA_BLOCK
    hd = w // N_HEADS
    assert hd == LANES and t % tq == 0
    full = pl.BlockSpec((None, t, w), lambda i, j: (i, 0, 0))
    tile = pl.BlockSpec((None, tq, w), lambda i, j: (i, j, 0))
    return pl.pallas_call(
        functools.partial(_moba_kernel, scale=hd ** -0.5),
        grid=(b, t // tq),
        in_specs=[tile, full, full, pl.BlockSpec((None, nb, w), lambda i, j: (i, 0, 0))],
        out_specs=tile,
        out_shape=jax.ShapeDtypeStruct((b, t, w), BF16),
        compiler_params=_params("arbitrary", "arbitrary"),
        name="moba",
    )(q, k, v, kmean)


def _sb_block(qh, kb, vb, tri, carry, scale, mask):
    c, acc = carry
    z = _dot_nt(qh, kb) * scale
    sp = _softplus_neg_abs(z)
    log_beta = jnp.minimum(z, 0.0) - sp
    log_rem = -jnp.maximum(z, 0.0) - sp
    if mask is not None:
        log_rem = jnp.where(mask, log_rem, 0.0)
    hi, lo = _split_hi_lo(log_rem)
    incl = _dot(hi, tri) + _dot(lo, tri)
    w = jnp.exp(log_beta + (incl - log_rem) + c)
    if mask is not None:
        w = jnp.where(mask, w, 0.0)
    acc = acc + _dot(w.astype(BF16), vb)
    return c + incl[:, 0:1], acc


def _sb_kernel(q_ref, k_ref, v_ref, o_ref, *, scale):
    i = pl.program_id(1)
    tq = q_ref.shape[0]
    hd = LANES
    r = lax.broadcasted_iota(jnp.int32, (tq, tq), 0)
    c = lax.broadcasted_iota(jnp.int32, (tq, tq), 1)
    tri = (r >= c).astype(BF16)
    strict = c < r
    diag_rows = pl.ds(pl.multiple_of(i * tq, tq), tq)
    for h in range(N_HEADS):
        hs = slice(h * hd, (h + 1) * hd)
        qh = q_ref[:, hs]
        carry = (jnp.zeros((tq, 1), F32), jnp.zeros((tq, hd), F32))
        carry = _sb_block(qh, k_ref[diag_rows, hs], v_ref[diag_rows, hs], tri, carry, scale, strict)

        def body(t, carry):
            n = i - 1 - t
            rows = pl.ds(pl.multiple_of(n * tq, tq), tq)
            return _sb_block(qh, k_ref[rows, hs], v_ref[rows, hs], tri, carry, scale, None)

        _, acc = lax.fori_loop(0, i, body, carry)
        o_ref[:, hs] = acc.astype(o_ref.dtype)


def _sb(q, k, v):
    b, t, w = q.shape
    tq = min(256, t)
    hd = w // N_HEADS
    assert hd == LANES and t % tq == 0
    full = pl.BlockSpec((None, t, w), lambda i, j: (i, 0, 0))
    tile = pl.BlockSpec((None, tq, w), lambda i, j: (i, j, 0))
    return pl.pallas_call(
        functools.partial(_sb_kernel, scale=hd ** -0.5),
        grid=(b, t // tq),
        in_specs=[tile, full, full],
        out_specs=tile,
        out_shape=jax.ShapeDtypeStruct((b, t, w), BF16),
        compiler_params=_params("arbitrary", "arbitrary"),
        name="stickbreak",
    )(q, k, v)


def _merge_kernel(x_ref, sh_ref, sc_ref, gt_ref, gpre_ref, gpost_ref, oa_ref, ob_ref, oc_ref,
                  w2_ref, wa_ref, wb_ref, wc_ref, wo_ref, o_ref, *, dv_total):
    x = x_ref[...]
    d = x.shape[-1]
    h = (_rms(x, gpre_ref[...]) * (1.0 + sc_ref[...]) + sh_ref[...]).astype(BF16)
    gr = _dot(h, w2_ref[:, :dv_total])
    oa = (oa_ref[...] * _silu(gr)).astype(BF16)
    merged = jax.nn.sigmoid(_dot(h, w2_ref[:, dv_total:dv_total + d])) * _dot(oa, wa_ref[...])
    merged = merged + (jax.nn.sigmoid(_dot(h, w2_ref[:, dv_total + d:dv_total + 2 * d]))
                       * _dot(ob_ref[...], wb_ref[...]))
    merged = merged + (jax.nn.sigmoid(_dot(h, w2_ref[:, dv_total + 2 * d:dv_total + 3 * d]))
                       * _dot(oc_ref[...], wc_ref[...]))
    m = _dot(merged.astype(BF16), wo_ref[...])
    o_ref[...] = x + gt_ref[...] * _rms(m, gpost_ref[...])


def _merge(x, sh, sc, gt, g_pre, g_post, oa, ob, oc, w2, wa, wb, wc, wo, *, tm, rows_per_group):
    n, d = x.shape
    dv_total = oa.shape[1]
    row = lambda w: pl.BlockSpec((tm, w), lambda i: (i, 0))
    vec = pl.BlockSpec((1, d), lambda i: (0, 0))
    res = lambda a: _resident(a.shape, lambda i: (0, 0))
    return pl.pallas_call(
        functools.partial(_merge_kernel, dv_total=dv_total),
        grid=(n // tm,),
        in_specs=[row(d), _mod_spec(sh, tm, rows_per_group), _mod_spec(sc, tm, rows_per_group),
                  _mod_spec(gt, tm, rows_per_group), vec, vec,
                  row(dv_total), row(ob.shape[1]), row(oc.shape[1]),
                  res(w2), res(wa), res(wb), res(wc), res(wo)],
        out_specs=row(d),
        out_shape=jax.ShapeDtypeStruct((n, d), F32),
        compiler_params=_params("arbitrary"),
        name="merge",
    )(x, sh, sc, gt, g_pre, g_post, oa, ob, oc, w2, wa, wb, wc, wo)


def _dec_gate_kernel(pt_ref, *refs, pages_per_step, pages_per_block, n_steps):
    page_refs = refs[:pages_per_step]
    q_ref = refs[pages_per_step]
    idx_ref = refs[pages_per_step + 1]
    km_ref = refs[pages_per_step + 2]
    j = pl.program_id(1)
    blocks_per_step = pages_per_step // pages_per_block
    inv = 1.0 / (pages_per_block * page_refs[0].shape[0])
    rows = []
    for blk in range(blocks_per_step):
        s = None
        for r in range(pages_per_block):
            part = jnp.sum(page_refs[blk * pages_per_block + r][...], axis=0, keepdims=True)
            s = part if s is None else s + part
        rows.append(s * inv)
    km_ref[pl.ds(pl.multiple_of(j * blocks_per_step, blocks_per_step), blocks_per_step), :] = (
        jnp.concatenate(rows, axis=0))

    @pl.when(j == n_steps - 1)
    def _():
        nb = km_ref.shape[0]
        tq = q_ref.shape[0]
        blk = lax.broadcasted_iota(jnp.int32, (tq, nb), 1)
        lane = lax.broadcasted_iota(jnp.int32, (tq, LANES), 1)
        out = jnp.zeros((tq, LANES), jnp.int32)
        for h in range(N_HEADS):
            hs = slice(h * LANES, (h + 1) * LANES)
            q_hi, q_lo = _split_hi_lo(q_ref[:, hs])
            k_hi, k_lo = _split_hi_lo(km_ref[:, hs])
            g = _dot_nt(q_hi, k_hi) + _dot_nt(q_hi, k_lo) + _dot_nt(q_lo, k_hi)
            for r in range(MOBA_TOPK):
                m = jnp.max(g, axis=-1, keepdims=True)
                idx = jnp.min(jnp.where(g == m, blk, nb), axis=-1, keepdims=True)
                out = jnp.where(lane == h * 4 + r, idx, out)
                g = jnp.where(blk == idx, NEG_INF, g)
        idx_ref[...] = out


def _dec_gate(cache_k, layer, page_table, q_pad):
    _, _, page, w = cache_k.shape
    db, n_pages = page_table.shape
    pages_per_block = MOBA_BLOCK // page
    blocks_per_step = 8
    pages_per_step = blocks_per_step * pages_per_block
    assert n_pages % pages_per_step == 0
    n_steps = n_pages // pages_per_step
    nb = n_pages // pages_per_block
    assert nb >= MOBA_TOPK

    def page_spec(r):
        return pl.BlockSpec((None, None, page, w),
                            lambda b, j, pt: (layer, pt[b, j * pages_per_step + r], 0, 0))

    grid_spec = pltpu.PrefetchScalarGridSpec(
        num_scalar_prefetch=1,
        grid=(db, n_steps),
        in_specs=[page_spec(r) for r in range(pages_per_step)]
        + [pl.BlockSpec((None, q_pad.shape[1], w), lambda b, j, pt: (b, 0, 0))],
        out_specs=pl.BlockSpec((None, q_pad.shape[1], LANES), lambda b, j, pt: (b, 0, 0)),
        scratch_shapes=[pltpu.VMEM((nb, w), F32)],
    )
    return pl.pallas_call(
        functools.partial(_dec_gate_kernel, pages_per_step=pages_per_step,
                          pages_per_block=pages_per_block, n_steps=n_steps),
        grid_spec=grid_spec,
        out_shape=jax.ShapeDtypeStruct((db, q_pad.shape[1], LANES), jnp.int32),
        compiler_params=_params("arbitrary", "arbitrary"),
        name="moba_dec_gate",
    )(page_table, *([cache_k] * pages_per_step), q_pad)


def _dec_moba_kernel(idx_ref, pt_ref, *refs, n_tok, pages_per_block, scale):
    n_sel = n_tok * MOBA_TOPK * pages_per_block
    k_refs = refs[:n_sel]
    v_refs = refs[n_sel:2 * n_sel]
    q_ref, kn_ref, vn_ref, o_ref = refs[2 * n_sel:]
    q = q_ref[...].astype(BF16)
    tq = q.shape[0]
    page = k_refs[0].shape[0]
    row = lax.broadcasted_iota(jnp.int32, (tq, page), 0)
    col = lax.broadcasted_iota(jnp.int32, (tq, page), 1)
    scores = [jnp.where(jnp.logical_and(col <= row, col < n_tok),
                        _dot_nt(q, kn_ref[...].astype(BF16)) * scale, NEG_INF)]
    values = [vn_ref[...].astype(BF16)]
    i = 0
    for t in range(n_tok):
        for _ in range(MOBA_TOPK * pages_per_block):
            s = _dot_nt(q, k_refs[i][...].astype(BF16)) * scale
            scores.append(jnp.where(row == t, s, NEG_INF))
            values.append(v_refs[i][...].astype(BF16))
            i += 1
    m = scores[0].max(axis=-1, keepdims=True)
    for s in scores[1:]:
        m = jnp.maximum(m, s.max(axis=-1, keepdims=True))
    l = jnp.zeros((tq, 1), F32)
    acc = jnp.zeros((tq, LANES), F32)
    for s, v in zip(scores, values):
        p = jnp.exp(s - m)
        l = l + jnp.sum(p, axis=-1, keepdims=True)
        acc = acc + _dot(p.astype(BF16), v)
    o_ref[...] = (acc / l).astype(o_ref.dtype)


def _dec_moba(cache_k, cache_v, layer, page_table, sel_flat, q_pad, k_new_pad, v_new_pad, *, n_tok):
    _, _, page, w = cache_k.shape
    nh = N_HEADS
    hd = w // nh
    db = page_table.shape[0]
    pages_per_block = MOBA_BLOCK // page
    tq = q_pad.shape[1]

    def page_spec(t, r, half):
        def index(b, h, sel, pt):
            n = sel[((b * n_tok + t) * nh + h) * MOBA_TOPK + r]
            return (layer, pt[b, n * pages_per_block + half], 0, h)
        return pl.BlockSpec((None, None, page, hd), index)

    sel_specs = [page_spec(t, r, half) for t in range(n_tok) for r in range(MOBA_TOPK)
                 for half in range(pages_per_block)]
    head_rows = lambda rows: pl.BlockSpec((None, rows, hd), lambda b, h, sel, pt: (b, 0, h))
    grid_spec = pltpu.PrefetchScalarGridSpec(
        num_scalar_prefetch=2,
        grid=(db, nh),
        in_specs=sel_specs + sel_specs + [head_rows(tq), head_rows(page), head_rows(page)],
        out_specs=head_rows(tq),
    )
    n_sel = len(sel_specs)
    return pl.pallas_call(
        functools.partial(_dec_moba_kernel, n_tok=n_tok, pages_per_block=pages_per_block,
                          scale=hd ** -0.5),
        grid_spec=grid_spec,
        out_shape=jax.ShapeDtypeStruct((db, tq, nh * hd), BF16),
        compiler_params=_params("arbitrary", "arbitrary"),
        name="moba_dec_attend",
    )(sel_flat, page_table, *([cache_k] * n_sel), *([cache_v] * n_sel), q_pad, k_new_pad, v_new_pad)


def _dec_sb_page(kp, vp, qblk, tri, c_ref, acc_ref, scale, mask):
    z = _dot(kp.astype(BF16), qblk) * scale
    sp = _softplus_neg_abs(z)
    log_beta = jnp.minimum(z, 0.0) - sp
    log_rem = -jnp.maximum(z, 0.0) - sp
    if mask is not None:
        log_rem = jnp.where(mask, log_rem, 0.0)
    hi, lo = _split_hi_lo(log_rem)
    incl = _dot(tri, hi) + _dot(tri, lo)
    c = c_ref[0:1, :]
    w = jnp.exp(log_beta + (incl - log_rem) + c)
    if mask is not None:
        w = jnp.where(mask, w, 0.0)
    acc_ref[...] += _dot_tn(w.astype(BF16), vp.astype(BF16))
    c_ref[0:1, :] = c + incl[0:1, :]


def _dec_sb_kernel(pt_ref, *refs, pages_per_step, n_steps, n_tok, scale):
    k_refs = refs[:pages_per_step]
    v_refs = refs[pages_per_step:2 * pages_per_step]
    qblk_ref, kn_ref, vn_ref, o_ref, c_ref, acc_ref = refs[2 * pages_per_step:]
    j = pl.program_id(1)
    page = kn_ref.shape[0]
    r = lax.broadcasted_iota(jnp.int32, (page, page), 0)
    cc = lax.broadcasted_iota(jnp.int32, (page, page), 1)
    tri = (cc >= r).astype(BF16)
    qblk = qblk_ref[...]

    @pl.when(j == 0)
    def _():
        c_ref[...] = jnp.zeros_like(c_ref)
        acc_ref[...] = jnp.zeros_like(acc_ref)
        key = lax.broadcasted_iota(jnp.int32, (page, LANES), 0)
        lane = lax.broadcasted_iota(jnp.int32, (page, LANES), 1)
        mask = jnp.logical_and(key < lane % n_tok, lane < N_HEADS * n_tok)
        _dec_sb_page(kn_ref[...], vn_ref[...], qblk, tri, c_ref, acc_ref, scale, mask)

    for p in range(pages_per_step):
        _dec_sb_page(k_refs[p][...], v_refs[p][...], qblk, tri, c_ref, acc_ref, scale, None)

    @pl.when(j == n_steps - 1)
    def _():
        o_ref[...] = acc_ref[0:o_ref.shape[0], :]


def _dec_sb(cache_k, cache_v, layer, page_table, qblk, k_new_pad, v_new_pad, *, n_tok):
    _, _, page, w = cache_k.shape
    db, n_pages = page_table.shape
    pages_per_step = 8
    assert n_pages % pages_per_step == 0 and page == LANES
    n_steps = n_pages // pages_per_step
    rows_out = 16
    assert N_HEADS * n_tok <= rows_out

    def page_spec(r):
        return pl.BlockSpec((None, None, page, w),
                            lambda b, j, pt: (layer, pt[b, n_pages - 1 - (j * pages_per_step + r)], 0, 0))

    per_b = lambda rows, cols: pl.BlockSpec((None, rows, cols), lambda b, j, pt: (b, 0, 0))
    grid_spec = pltpu.PrefetchScalarGridSpec(
        num_scalar_prefetch=1,
        grid=(db, n_steps),
        in_specs=[page_spec(r) for r in range(pages_per_step)] * 2
        + [per_b(w, LANES), per_b(page, w), per_b(page, w)],
        out_specs=per_b(rows_out, w),
        scratch_shapes=[pltpu.VMEM((8, LANES), F32), pltpu.VMEM((LANES, w), F32)],
    )
    return pl.pallas_call(
        functools.partial(_dec_sb_kernel, pages_per_step=pages_per_step, n_steps=n_steps,
                          n_tok=n_tok, scale=(w // N_HEADS) ** -0.5),
        grid_spec=grid_spec,
        out_shape=jax.ShapeDtypeStruct((db, rows_out, w), F32),
        compiler_params=_params("arbitrary", "arbitrary"),
        name="sb_dec",
    )(page_table, *([cache_k] * pages_per_step), *([cache_v] * pages_per_step),
      qblk, k_new_pad, v_new_pad)


def _rope_tables(pos, hd):
    rope_dim = hd // 4
    half = rope_dim // 2
    inv = ROPE_THETA ** (-jnp.arange(half, dtype=F32) / half)
    ang = pos.astype(F32)[:, None] * inv[None, :]
    cos, sin = jnp.cos(ang), jnp.sin(ang)
    n = pos.shape[0]
    ones = jnp.ones((n, hd - rope_dim), F32)
    cos_t = jnp.concatenate([cos, cos, ones], axis=-1)
    sin_t = jnp.concatenate([-sin, sin, jnp.zeros_like(ones)], axis=-1)
    return cos_t, sin_t


def kernel(x_prompt, x_sample, cache_moba_k, cache_moba_v, cache_sb_k, cache_sb_v, state_gla, page_table, c_prompt, c_sample, w_ada, b_ada, g_ffn1_pre, g_ffn1_post, g_mix_pre, g_mix_post, g_ffn2_pre, g_ffn2_post, w_ffn1_in, w_ffn1_down, w_ffn2_in, w_ffn2_down, w_in, w_gla_alpha, b_gla_alpha, g_gla_head, w_br_gla, w_br_moba, w_br_sb, w_out):
    bsz, t_p, d = x_prompt.shape
    db, t_s, _ = x_sample.shape
    depth = w_ada.shape[0]
    _, n_phys, page, nh, hd = cache_moba_k.shape
    _, _, _, dk, dv = state_gla.shape
    rank = w_gla_alpha.shape[1]
    assert nh == N_HEADS
    dk_total, dv_total, hd_total = nh * dk, nh * dv, nh * hd
    past_len = page_table.shape[1] * page
    n_p, n_s = bsz * t_p, db * t_s
    tm_p = min(512, t_p)
    t_pad = 8
    assert t_s <= t_pad and t_p % tm_p == 0

    mod = _ada(jnp.concatenate([c_prompt, c_sample], axis=0), w_ada, b_ada)

    o_gr = 2 * dk_total + dv_total
    o_glr = o_gr + dv_total
    o_m = o_glr + rank
    o_s = o_m + 3 * hd_total
    o_g = o_s + 3 * hd_total

    cos_p, sin_p = _rope_tables(jnp.arange(t_p, dtype=jnp.int32), hd)
    cos_s, sin_s = _rope_tables(past_len + jnp.arange(t_s, dtype=jnp.int32), hd)
    cos_s, sin_s = jnp.tile(cos_s, (db, 1)), jnp.tile(sin_s, (db, 1))

    yp = x_prompt.reshape(n_p, d)
    ys = x_sample.reshape(n_s, d)
    cache_mk4 = cache_moba_k.reshape(depth, n_phys, page, hd_total)
    cache_mv4 = cache_moba_v.reshape(depth, n_phys, page, hd_total)
    cache_sk4 = cache_sb_k.reshape(depth, n_phys, page, hd_total)
    cache_sv4 = cache_sb_v.reshape(depth, n_phys, page, hd_total)

    outs_p = [[] for _ in range(5)]
    outs_s = [[] for _ in range(5)]
    for l in range(depth):
        vec = lambda a: a[l].reshape(1, -1)
        w1 = jnp.concatenate(
            [w_in[l][:, :o_gr], jnp.pad(w_in[l][:, o_glr:o_m], ((0, 0), (0, LANES - rank))),
             w_in[l][:, o_m:o_g]], axis=1).astype(BF16)
        w2 = jnp.concatenate([w_in[l][:, o_gr:o_glr], w_in[l][:, o_g:]], axis=1).astype(BF16)
        w_alpha = jnp.pad(w_gla_alpha[l], ((0, LANES - rank), (0, 0))).astype(BF16)
        wf1_in, wf1_dn = w_ffn1_in[l].astype(BF16), w_ffn1_down[l].astype(BF16)
        wf2_in, wf2_dn = w_ffn2_in[l].astype(BF16), w_ffn2_down[l].astype(BF16)
        wa, wb, wc = w_br_gla[l].astype(BF16), w_br_moba[l].astype(BF16), w_br_sb[l].astype(BF16)
        wo = w_out[l].astype(BF16)
        mods = jnp.split(mod[l], N_MOD, axis=-1)
        mod_p = [m[:bsz].reshape(bsz, 1, d) for m in mods]
        mod_s = [jnp.repeat(m[bsz:], t_s, axis=0).reshape(1, n_s, d) for m in mods]

        kw = dict(tm=tm_p, rows_per_group=t_p)
        yp = _ffn(yp, mod_p[0], mod_p[1], mod_p[2], vec(g_ffn1_pre), vec(g_ffn1_post), wf1_in, wf1_dn, **kw)
        (gq, gk, gv, la, mq, mk, mkb, mv, mvb, sq, sk, skb, sv, svb) = _mixin(
            yp, mod_p[3], mod_p[4], vec(g_mix_pre), cos_p, sin_p, w1, w_alpha, vec(b_gla_alpha),
            dk_total=dk_total, dv_total=dv_total, hd_total=hd_total, **kw)
        seq = lambda a: a.reshape(bsz, t_p, a.shape[-1])
        o_a, st = _gla(seq(gq), seq(gk), seq(la), seq(gv), jnp.zeros((bsz, nh, dk, dv), F32),
                       vec(g_gla_head), tt=tm_p, chunk=math.gcd(t_p, GLA_CHUNK))
        o_b = _moba(seq(mq), seq(mkb), seq(mvb), _kmean(seq(mk)))
        o_c = _sb(seq(sq), seq(skb), seq(svb))
        yp = _merge(yp, mod_p[3], mod_p[4], mod_p[5], vec(g_mix_pre), vec(g_mix_post),
                    o_a.reshape(n_p, dv_total), o_b.reshape(n_p, hd_total), o_c.reshape(n_p, hd_total),
                    w2, wa, wb, wc, wo, **kw)
        yp = _ffn(yp, mod_p[6], mod_p[7], mod_p[8], vec(g_ffn2_pre), vec(g_ffn2_post), wf2_in, wf2_dn, **kw)
        for lst, a in zip(outs_p, (mk, mv, sk, sv)):
            lst.append(a.reshape(bsz, t_p, nh, hd))
        outs_p[4].append(st)

        kw = dict(tm=n_s, rows_per_group=n_s)
        ys = _ffn(ys, mod_s[0], mod_s[1], mod_s[2], vec(g_ffn1_pre), vec(g_ffn1_post), wf1_in, wf1_dn, **kw)
        (gq, gk, gv, la, mq, mk, mkb, mv, mvb, sq, sk, skb, sv, svb) = _mixin(
            ys, mod_s[3], mod_s[4], vec(g_mix_pre), cos_s, sin_s, w1, w_alpha, vec(b_gla_alpha),
            dk_total=dk_total, dv_total=dv_total, hd_total=hd_total, **kw)
        seq = lambda a: a.reshape(db, t_s, a.shape[-1])
        pad_t = lambda a, rows: jnp.pad(seq(a), ((0, 0), (0, rows - t_s), (0, 0)))
        o_a, st = _gla(pad_t(gq, t_pad), pad_t(gk, t_pad), pad_t(la, t_pad), pad_t(gv, t_pad),
                       state_gla[l], vec(g_gla_head), tt=t_pad, chunk=t_pad)
        o_a = o_a[:, :t_s]
        mq_pad = pad_t(mq.astype(F32), t_pad)
        sel = _dec_gate(cache_mk4, l, page_table, mq_pad)
        sel_flat = sel[:, :t_s, :nh * 4].reshape(db, t_s, nh, 4)[..., :MOBA_TOPK].reshape(-1)
        o_b = _dec_moba(cache_mk4, cache_mv4, l, page_table, sel_flat, mq_pad,
                        pad_t(mk, page), pad_t(mv, page), n_tok=t_s)[:, :t_s]
        q4 = seq(sq).reshape(db, t_s, nh, hd)
        eye = jnp.eye(nh, dtype=BF16)
        qblk = jnp.einsum("bthd,hg->bhdgt", q4, eye).reshape(db, hd_total, nh * t_s)
        qblk = jnp.pad(qblk, ((0, 0), (0, 0), (0, LANES - nh * t_s)))
        o_raw = _dec_sb(cache_sk4, cache_sv4, l, page_table, qblk, pad_t(sk, page), pad_t(sv, page), n_tok=t_s)
        o_c = jnp.stack([o_raw[:, h * t_s:(h + 1) * t_s, h * hd:(h + 1) * hd] for h in range(nh)], axis=2)
        o_c = o_c.reshape(n_s, hd_total).astype(BF16)
        ys = _merge(ys, mod_s[3], mod_s[4], mod_s[5], vec(g_mix_pre), vec(g_mix_post),
                    o_a.reshape(n_s, dv_total), o_b.reshape(n_s, hd_total), o_c,
                    w2, wa, wb, wc, wo, **kw)
        ys = _ffn(ys, mod_s[6], mod_s[7], mod_s[8], vec(g_ffn2_pre), vec(g_ffn2_post), wf2_in, wf2_dn, **kw)
        for lst, a in zip(outs_s, (mk, mv, sk, sv)):
            lst.append(a.reshape(db, t_s, nh, hd))
        outs_s[4].append(st)

    stack = lambda lst: jnp.stack(lst)
    return (yp.reshape(bsz, t_p, d), ys.reshape(db, t_s, d),
            stack(outs_p[0]), stack(outs_p[1]), stack(outs_p[2]), stack(outs_p[3]), stack(outs_p[4]),
            stack(outs_s[0]), stack(outs_s[1]), stack(outs_s[2]), stack(outs_s[3]), stack(outs_s[4]))
```

```python
import functools
import math

import jax
import jax.numpy as jnp
from jax import lax
from jax.experimental import pallas as pl
from jax.experimental.pallas import tpu as pltpu

F32 = jnp.float32
BF16 = jnp.bfloat16

EPS = 1e-6
N_HEADS = 4
GLA_TAU = 16.0
GLA_CHUNK = 64
MOBA_BLOCK = 256
MOBA_TOPK = 3
ROPE_THETA = 500000.0
N_MOD = 9
LANES = 128
VMEM_LIMIT_BYTES = 56 * 1024 * 1024

NEG_INF = float("-inf")


def _dot(a, b):
    return jnp.dot(a, b, preferred_element_type=F32)


def _dot_nt(a, b):
    return lax.dot_general(a, b, (((1,), (1,)), ((), ())), preferred_element_type=F32)


def _dot_tn(a, b):
    return lax.dot_general(a, b, (((0,), (0,)), ((), ())), preferred_element_type=F32)


def _rms(x, g):
    return x * lax.rsqrt(jnp.mean(x * x, axis=-1, keepdims=True) + EPS) * g


def _silu(x):
    return x * jax.nn.sigmoid(x)


def _softplus_neg_abs(z):
    return jnp.log1p(jnp.exp(-jnp.abs(z)))


def _split_hi_lo(x):
    hi = x.astype(BF16)
    lo = (x - hi.astype(F32)).astype(BF16)
    return hi, lo


def _params(*sem):
    return pltpu.CompilerParams(dimension_semantics=sem, vmem_limit_bytes=VMEM_LIMIT_BYTES)


def _resident(shape, index_map):
    return pl.BlockSpec(shape, index_map, pipeline_mode=pl.Buffered(1))


def _mod_spec(mod, tm, rows_per_group):
    g, r, d = mod.shape
    if r == 1:
        tiles_per_group = rows_per_group // tm
        return pl.BlockSpec((None, 1, d), lambda i: (i // tiles_per_group, 0, 0))
    assert g == 1 and r == tm
    return pl.BlockSpec((None, r, d), lambda i: (0, 0, 0))


def _ada_kernel(c_ref, w_ref, b_ref, o_ref):
    a = _silu(c_ref[...]).astype(BF16)
    o_ref[...] = _dot(a, w_ref[...].astype(BF16)) + b_ref[...]


def _ada(c_all, w_ada, b_ada):
    depth, d, n = w_ada.shape
    m = c_all.shape[0]
    tn = 1024
    return pl.pallas_call(
        _ada_kernel,
        grid=(depth, n // tn),
        in_specs=[
            pl.BlockSpec((m, d), lambda l, j: (0, 0)),
            pl.BlockSpec((None, d, tn), lambda l, j: (l, 0, j)),
            pl.BlockSpec((None, 1, tn), lambda l, j: (l, 0, j)),
        ],
        out_specs=pl.BlockSpec((None, m, tn), lambda l, j: (l, 0, j)),
        out_shape=jax.ShapeDtypeStruct((depth, m, n), F32),
        compiler_params=_params("arbitrary", "arbitrary"),
        name="ada_mod",
    )(c_all, w_ada, b_ada.reshape(depth, 1, n))


def _ffn_kernel(x_ref, sh_ref, sc_ref, gt_ref, gpre_ref, gpost_ref, wg_ref, wu_ref, wd_ref,
                o_ref, a_ref, *, tf):
    x = x_ref[...]
    h = (_rms(x, gpre_ref[...]) * (1.0 + sc_ref[...]) + sh_ref[...]).astype(BF16)
    f = wd_ref.shape[0]
    for c in range(f // tf):
        sl = slice(c * tf, (c + 1) * tf)
        g = _dot(h, wg_ref[:, sl])
        u = _dot(h, wu_ref[:, sl])
        a_ref[:, sl] = (_silu(g) * u).astype(BF16)
    y = _dot(a_ref[...], wd_ref[...])
    o_ref[...] = x + 0.5 * gt_ref[...] * _rms(y, gpost_ref[...])


def _ffn(x, sh, sc, gt, g_pre, g_post, w_in, w_down, *, tm, rows_per_group):
    n, d = x.shape
    f = w_down.shape[0]
    tf = 256
    assert f % tf == 0 and n % tm == 0
    row = pl.BlockSpec((tm, d), lambda i: (i, 0))
    vec = pl.BlockSpec((1, d), lambda i: (0, 0))
    return pl.pallas_call(
        functools.partial(_ffn_kernel, tf=tf),
        grid=(n // tm,),
        in_specs=[row, _mod_spec(sh, tm, rows_per_group), _mod_spec(sc, tm, rows_per_group),
                  _mod_spec(gt, tm, rows_per_group), vec, vec,
                  _resident((d, f), lambda i: (0, 0)), _resident((d, f), lambda i: (0, 1)),
                  _resident((f, d), lambda i: (0, 0))],
        out_specs=row,
        out_shape=jax.ShapeDtypeStruct((n, d), F32),
        scratch_shapes=[pltpu.VMEM((tm, f), BF16)],
        compiler_params=_params("arbitrary"),
        name="ffn",
    )(x, sh, sc, gt, g_pre, g_post, w_in, w_in, w_down)


def _rope(x, cos_t, sin_t):
    quarter = cos_t.shape[-1] // 8
    lane = lax.broadcasted_iota(jnp.int32, cos_t.shape, 1)
    outs = []
    for h in range(x.shape[-1] // LANES):
        xh = x[:, h * LANES:(h + 1) * LANES]
        partner = jnp.where(lane < quarter, pltpu.roll(xh, LANES - quarter, 1), pltpu.roll(xh, quarter, 1))
        outs.append(xh * cos_t + partner * sin_t)
    return jnp.concatenate(outs, axis=-1)


def _store_token_head_rows(ref, x):
    tm = x.shape[0]
    for h in range(N_HEADS):
        ref[pl.ds(h, tm, stride=N_HEADS), :] = x[:, h * LANES:(h + 1) * LANES]


def _mixin_kernel(x_ref, sh_ref, sc_ref, g_ref, cos_ref, sin_ref, w_ref, wa_ref, ba_ref,
                  gq_ref, gk_ref, gv_ref, la_ref,
                  mq_ref, mk_ref, mkb_ref, mv_ref, mvb_ref,
                  sq_ref, sk_ref, skb_ref, sv_ref, svb_ref, *, dk_total, dv_total, hd_total):
    x = x_ref[...]
    h = (_rms(x, g_ref[...]) * (1.0 + sc_ref[...]) + sh_ref[...]).astype(BF16)
    o = 0
    qk = _dot(h, w_ref[:, o:o + 2 * dk_total]); o += 2 * dk_total
    dk = dk_total // N_HEADS
    gq_ref[...] = qk[:, :dk_total] * (dk ** -0.5)
    gk_ref[...] = qk[:, dk_total:]
    gv_ref[...] = _dot(h, w_ref[:, o:o + dv_total]); o += dv_total
    glr = _dot(h, w_ref[:, o:o + LANES]); o += LANES
    alpha = _dot(glr.astype(BF16), wa_ref[...]) + ba_ref[...]
    la_ref[...] = (jnp.minimum(alpha, 0.0) - _softplus_neg_abs(alpha)) / GLA_TAU
    cos_t = cos_ref[...]
    sin_t = sin_ref[...]
    mq = _rope(_dot(h, w_ref[:, o:o + hd_total]), cos_t, sin_t); o += hd_total
    mq_ref[...] = mq.astype(BF16)
    mk = _rope(_dot(h, w_ref[:, o:o + hd_total]), cos_t, sin_t); o += hd_total
    _store_token_head_rows(mk_ref, mk)
    mkb_ref[...] = mk.astype(BF16)
    mv = _dot(h, w_ref[:, o:o + hd_total]); o += hd_total
    _store_token_head_rows(mv_ref, mv)
    mvb_ref[...] = mv.astype(BF16)
    sq_ref[...] = _dot(h, w_ref[:, o:o + hd_total]).astype(BF16); o += hd_total
    sk = _dot(h, w_ref[:, o:o + hd_total]); o += hd_total
    _store_token_head_rows(sk_ref, sk)
    skb_ref[...] = sk.astype(BF16)
    sv = _dot(h, w_ref[:, o:o + hd_total]); o += hd_total
    _store_token_head_rows(sv_ref, sv)
    svb_ref[...] = sv.astype(BF16)


def _mixin(x, sh, sc, g_pre, cos_t, sin_t, w1, w_alpha, b_alpha, *, tm, rows_per_group,
           dk_total, dv_total, hd_total):
    n, d = x.shape
    width = w1.shape[1]
    table_tiles = cos_t.shape[0] // tm
    row = lambda w: pl.BlockSpec((tm, w), lambda i: (i, 0))
    vec = lambda w: pl.BlockSpec((1, w), lambda i: (0, 0))
    table = pl.BlockSpec((tm, LANES), lambda i: (i % table_tiles, 0))
    hd = hd_total // N_HEADS
    outs = [(1, dk_total, F32), (1, dk_total, F32), (1, dv_total, F32), (1, dk_total, F32),
            (1, hd_total, BF16), (N_HEADS, hd, F32), (1, hd_total, BF16), (N_HEADS, hd, F32), (1, hd_total, BF16),
            (1, hd_total, BF16), (N_HEADS, hd, F32), (1, hd_total, BF16), (N_HEADS, hd, F32), (1, hd_total, BF16)]
    return pl.pallas_call(
        functools.partial(_mixin_kernel, dk_total=dk_total, dv_total=dv_total, hd_total=hd_total),
        grid=(n // tm,),
        in_specs=[row(d), _mod_spec(sh, tm, rows_per_group), _mod_spec(sc, tm, rows_per_group),
                  vec(d), table, table,
                  _resident((d, width), lambda i: (0, 0)),
                  _resident(w_alpha.shape, lambda i: (0, 0)), vec(dk_total)],
        out_specs=[pl.BlockSpec((tm * r, w), lambda i: (i, 0)) for r, w, _ in outs],
        out_shape=[jax.ShapeDtypeStruct((n * r, w), t) for r, w, t in outs],
        compiler_params=_params("arbitrary"),
        name="mixin",
    )(x, sh, sc, g_pre, cos_t, sin_t, w1, w_alpha, b_alpha)


def _cumsum_rows(x):
    n = x.shape[0]
    row = lax.broadcasted_iota(jnp.int32, x.shape, 0)
    s = 1
    while s < n:
        x = x + jnp.where(row >= s, pltpu.roll(x, s, 0), 0.0)
        s *= 2
    return x


def _gla_kernel(q_ref, k_ref, la_ref, v_ref, s0_ref, g_ref, o_ref, sout_ref, st_ref, *, chunk):
    j = pl.program_id(1)
    n_pairs = N_HEADS // 2
    dk = s0_ref.shape[1]

    @pl.when(j == 0)
    def _():
        for p in range(n_pairs):
            pair = jnp.concatenate([s0_ref[2 * p], s0_ref[2 * p + 1]], axis=0)
            st_ref[p] = pair.T

    tt = q_ref.shape[0]
    lane = lax.broadcasted_iota(jnp.int32, (1, LANES), 1)
    first = lane < dk
    tril = (lax.broadcasted_iota(jnp.int32, (chunk, chunk), 1)
            <= lax.broadcasted_iota(jnp.int32, (chunk, chunk), 0))
    g = g_ref[...]

    def body(c, carry):
        rs = pl.ds(pl.multiple_of(c * chunk, chunk), chunk)
        b = _cumsum_rows(la_ref[rs, :])
        b_last = b[chunk - 1:chunk, :]
        q = q_ref[rs, :]
        k = k_ref[rs, :]
        qe = q * jnp.exp(b)
        ki = k * jnp.exp(-b)
        kd = k * jnp.exp(b_last - b)
        eb = jnp.exp(b_last)
        for p in range(n_pairs):
            ls = slice(p * LANES, (p + 1) * LANES)
            st = st_ref[p]
            st_b = st.astype(BF16)
            ki_b = ki[:, ls].astype(BF16)
            upd = jnp.zeros_like(st)
            for hl in range(2):
                hh = 2 * p + hl
                msk = first if hl == 0 else jnp.logical_not(first)
                qm = jnp.where(msk, qe[:, ls], 0.0).astype(BF16)
                kdm = jnp.where(msk, kd[:, ls], 0.0).astype(BF16)
                vs = slice(hh * LANES, (hh + 1) * LANES)
                vh = v_ref[rs, vs].astype(BF16)
                att = jnp.where(tril, _dot_nt(qm, ki_b), 0.0)
                o = _dot_nt(qm, st_b) + _dot(att.astype(BF16), vh)
                o_ref[rs, vs] = _rms(o, g)
                upd = upd + _dot_tn(vh, kdm)
            st_ref[p] = st * eb[:, ls] + upd
        return carry

    lax.fori_loop(0, tt // chunk, body, 0)

    @pl.when(j == pl.num_programs(1) - 1)
    def _():
        for p in range(n_pairs):
            t = st_ref[p].T
            sout_ref[2 * p] = t[:dk]
            sout_ref[2 * p + 1] = t[dk:]


def _gla(q, k, la, v, s0, g_head, *, tt, chunk):
    b, t, dkt = q.shape
    dvt = v.shape[-1]
    _, nh, dk, dv = s0.shape
    assert nh == N_HEADS and 2 * dk == LANES and dv == LANES and t % tt == 0 and tt % chunk == 0
    seq = lambda w: pl.BlockSpec((None, tt, w), lambda i, j: (i, j, 0))
    st = pl.BlockSpec((None, nh, dk, dv), lambda i, j: (i, 0, 0, 0))
    return pl.pallas_call(
        functools.partial(_gla_kernel, chunk=chunk),
        grid=(b, t // tt),
        in_specs=[seq(dkt), seq(dkt), seq(dkt), seq(dvt), st, pl.BlockSpec((1, dv), lambda i, j: (0, 0))],
        out_specs=[seq(dvt), st],
        out_shape=[jax.ShapeDtypeStruct((b, t, dvt), F32), jax.ShapeDtypeStruct(s0.shape, F32)],
        scratch_shapes=[pltpu.VMEM((nh // 2, LANES, LANES), F32)],
        compiler_params=_params("arbitrary", "arbitrary"),
        name="gla",
    )(q, k, la, v, s0, g_head)


def _head_row_sums(x):
    sub = 8
    part = jnp.sum(x.reshape(x.shape[0] // sub, sub, x.shape[1]), axis=0)
    out = part[0:N_HEADS]
    for g in range(1, sub // N_HEADS):
        out = out + part[g * N_HEADS:(g + 1) * N_HEADS]
    return out


def _kmean_kernel(k_ref, o_ref):
    o_ref[...] = _head_row_sums(k_ref[...]) * (float(N_HEADS) / k_ref.shape[0])


def _kmean(k_rows, bsz):
    hd = k_rows.shape[1]
    t = k_rows.shape[0] // (bsz * N_HEADS)
    nb = t // MOBA_BLOCK
    rows = MOBA_BLOCK * N_HEADS
    out = pl.pallas_call(
        _kmean_kernel,
        grid=(bsz, nb),
        in_specs=[pl.BlockSpec((rows, hd), lambda i, j: (i * nb + j, 0))],
        out_specs=pl.BlockSpec((None, None, N_HEADS, hd), lambda i, j: (i, j, 0, 0)),
        out_shape=jax.ShapeDtypeStruct((bsz, nb, N_HEADS, hd), F32),
        compiler_params=_params("arbitrary", "arbitrary"),
        name="moba_kmean",
    )(k_rows)
    return out.transpose(0, 2, 1, 3)


def _top_blocks(gate, allowed, blk, n_blocks):
    g = jnp.where(allowed, gate, NEG_INF)
    sel = jnp.zeros(gate.shape, jnp.bool_)
    for _ in range(MOBA_TOPK):
        m = jnp.max(g, axis=-1, keepdims=True)
        is_m = jnp.logical_and(g == m, allowed)
        idx = jnp.min(jnp.where(is_m, blk, n_blocks), axis=-1, keepdims=True)
        pick = jnp.logical_and(blk == idx, is_m)
        sel = jnp.logical_or(sel, pick)
        g = jnp.where(pick, NEG_INF, g)
    return sel


def _moba_kernel(q_ref, k_ref, v_ref, km_ref, o_ref, m_ref, l_ref, acc_ref, sel_ref, *, scale):
    own = pl.program_id(1)
    tq = q_ref.shape[0]
    nb = km_ref.shape[1]
    hd = LANES
    heads = [slice(h * hd, (h + 1) * hd) for h in range(N_HEADS)]
    blk = lax.broadcasted_iota(jnp.int32, (tq, nb), 1)
    allowed = blk < own
    causal = (lax.broadcasted_iota(jnp.int32, (tq, tq), 1)
              <= lax.broadcasted_iota(jnp.int32, (tq, tq), 0))
    own_rows = pl.ds(pl.multiple_of(own * tq, tq), tq)
    for h, hs in enumerate(heads):
        qh = q_ref[:, hs]
        km_hi, km_lo = _split_hi_lo(km_ref[h])
        gate = _dot_nt(qh, km_hi) + _dot_nt(qh, km_lo)
        sel_ref[h] = _top_blocks(gate, allowed, blk, nb).astype(F32)
        s = jnp.where(causal, _dot_nt(qh, k_ref[own_rows, hs]) * scale, NEG_INF)
        m = jnp.max(s, axis=-1, keepdims=True)
        p = jnp.exp(s - m)
        m_ref[h] = m
        l_ref[h] = jnp.sum(p, axis=-1, keepdims=True)
        acc_ref[h] = _dot(p.astype(BF16), v_ref[own_rows, hs])

    def body(n, carry):
        rows = pl.ds(pl.multiple_of(n * tq, tq), tq)
        for h, hs in enumerate(heads):
            chosen = jnp.sum(jnp.where(blk == n, sel_ref[h], 0.0), axis=-1, keepdims=True) > 0.0
            s = jnp.where(chosen, _dot_nt(q_ref[:, hs], k_ref[rows, hs]) * scale, NEG_INF)
            m = m_ref[h]
            m_new = jnp.maximum(m, jnp.max(s, axis=-1, keepdims=True))
            a = jnp.exp(m - m_new)
            p = jnp.exp(s - m_new)
            m_ref[h] = m_new
            l_ref[h] = a * l_ref[h] + jnp.sum(p, axis=-1, keepdims=True)
            acc_ref[h] = a * acc_ref[h] + _dot(p.astype(BF16), v_ref[rows, hs])
        return carry

    lax.fori_loop(0, own, body, 0)
    for h, hs in enumerate(heads):
        o_ref[:, hs] = (acc_ref[h] / l_ref[h]).astype(o_ref.dtype)


def _moba(q, k, v, kmean):
    b, t, w = q.shape
    nb = kmean.shape[2]
    tq = MOBA_BLOCK
    hd = w // N_HEADS
    assert hd == LANES and t % tq == 0
    full = pl.BlockSpec((None, t, w), lambda i, j: (i, 0, 0))
    tile = pl.BlockSpec((None, tq, w), lambda i, j: (i, j, 0))
    return pl.pallas_call(
        functools.partial(_moba_kernel, scale=hd ** -0.5),
        grid=(b, t // tq),
        in_specs=[tile, full, full, pl.BlockSpec((None, N_HEADS, nb, hd), lambda i, j: (i, 0, 0, 0))],
        out_specs=tile,
        out_shape=jax.ShapeDtypeStruct((b, t, w), BF16),
        scratch_shapes=[pltpu.VMEM((N_HEADS, tq, 1), F32), pltpu.VMEM((N_HEADS, tq, 1), F32),
                        pltpu.VMEM((N_HEADS, tq, hd), F32), pltpu.VMEM((N_HEADS, tq, nb), F32)],
        compiler_params=_params("arbitrary", "arbitrary"),
        name="moba",
    )(q, k, v, kmean)


SB_EXP_ZERO = -104.0


def _sb_heads_block(get_q, get_k, get_v, tri_ext, c_ref, acc_ref, scale, mask):
    tk = tri_ext.shape[0]
    c_max = None
    for h in range(N_HEADS):
        z = _dot_nt(get_q(h), get_k(h)) * scale
        sp = _softplus_neg_abs(z)
        log_beta = jnp.minimum(z, 0.0) - sp
        log_rem = -jnp.maximum(z, 0.0) - sp
        if mask is not None:
            log_rem = jnp.where(mask, log_rem, 0.0)
        hi, lo = _split_hi_lo(log_rem)
        res = _dot(hi, tri_ext) + _dot(lo, tri_ext)
        incl = res[:, :tk]
        c_old = c_ref[h]
        w = jnp.exp(log_beta + (incl - log_rem) + jnp.tile(c_old, (1, tk // LANES)))
        if mask is not None:
            w = jnp.where(mask, w, 0.0)
        acc_ref[h] += _dot(w.astype(BF16), get_v(h))
        c_new = c_old + res[:, tk:]
        c_ref[h] = c_new
        m = jnp.max(c_new)
        c_max = m if c_max is None else jnp.maximum(c_max, m)
    return c_max


def _sb_kernel(q_ref, k_ref, v_ref, o_ref, acc_ref, c_ref, flag_ref, *, scale):
    i = pl.program_id(1)
    tq = q_ref.shape[0]
    hd = LANES
    r = lax.broadcasted_iota(jnp.int32, (tq, tq), 0)
    c = lax.broadcasted_iota(jnp.int32, (tq, tq), 1)
    tri_ext = jnp.concatenate([(r >= c).astype(BF16), jnp.ones((tq, LANES), BF16)], axis=1)
    strict = c < r
    get_q = lambda h: q_ref[:, h * hd:(h + 1) * hd]

    def block(n, mask):
        rows = pl.ds(pl.multiple_of(n * tq, tq), tq)
        c_max = _sb_heads_block(get_q, lambda h: k_ref[rows, h * hd:(h + 1) * hd],
                                lambda h: v_ref[rows, h * hd:(h + 1) * hd],
                                tri_ext, c_ref, acc_ref, scale, mask)
        flag_ref[0] = (c_max > SB_EXP_ZERO).astype(jnp.int32)

    acc_ref[...] = jnp.zeros_like(acc_ref)
    c_ref[...] = jnp.zeros_like(c_ref)
    block(i, strict)

    def body(t, carry):
        @pl.when(flag_ref[0] > 0)
        def _():
            block(i - 1 - t, None)
        return carry

    lax.fori_loop(0, i, body, 0)
    for h in range(N_HEADS):
        o_ref[:, h * hd:(h + 1) * hd] = acc_ref[h].astype(o_ref.dtype)


def _sb(q, k, v):
    b, t, w = q.shape
    tq = min(256, t)
    hd = w // N_HEADS
    assert hd == LANES and t % tq == 0 and tq % LANES == 0
    full = pl.BlockSpec((None, t, w), lambda i, j: (i, 0, 0))
    tile = pl.BlockSpec((None, tq, w), lambda i, j: (i, j, 0))
    return pl.pallas_call(
        functools.partial(_sb_kernel, scale=hd ** -0.5),
        grid=(b, t // tq),
        in_specs=[tile, full, full],
        out_specs=tile,
        out_shape=jax.ShapeDtypeStruct((b, t, w), BF16),
        scratch_shapes=[pltpu.VMEM((N_HEADS, tq, hd), F32), pltpu.VMEM((N_HEADS, tq, LANES), F32),
                        pltpu.SMEM((1,), jnp.int32)],
        compiler_params=_params("arbitrary", "arbitrary"),
        name="stickbreak",
    )(q, k, v)


def _merge_kernel(x_ref, sh_ref, sc_ref, gt_ref, gpre_ref, gpost_ref, oa_ref, ob_ref, oc_ref,
                  w2_ref, wa_ref, wb_ref, wc_ref, wo_ref, o_ref, *, dv_total):
    x = x_ref[...]
    d = x.shape[-1]
    h = (_rms(x, gpre_ref[...]) * (1.0 + sc_ref[...]) + sh_ref[...]).astype(BF16)
    gr = _dot(h, w2_ref[:, :dv_total])
    oa = (oa_ref[...] * _silu(gr)).astype(BF16)
    merged = jax.nn.sigmoid(_dot(h, w2_ref[:, dv_total:dv_total + d])) * _dot(oa, wa_ref[...])
    merged = merged + (jax.nn.sigmoid(_dot(h, w2_ref[:, dv_total + d:dv_total + 2 * d]))
                       * _dot(ob_ref[...], wb_ref[...]))
    merged = merged + (jax.nn.sigmoid(_dot(h, w2_ref[:, dv_total + 2 * d:dv_total + 3 * d]))
                       * _dot(oc_ref[...], wc_ref[...]))
    m = _dot(merged.astype(BF16), wo_ref[...])
    o_ref[...] = x + gt_ref[...] * _rms(m, gpost_ref[...])


def _merge(x, sh, sc, gt, g_pre, g_post, oa, ob, oc, w2, wa, wb, wc, wo, *, tm, rows_per_group):
    n, d = x.shape
    dv_total = oa.shape[1]
    row = lambda w: pl.BlockSpec((tm, w), lambda i: (i, 0))
    vec = pl.BlockSpec((1, d), lambda i: (0, 0))
    res = lambda a: _resident(a.shape, lambda i: (0, 0))
    return pl.pallas_call(
        functools.partial(_merge_kernel, dv_total=dv_total),
        grid=(n // tm,),
        in_specs=[row(d), _mod_spec(sh, tm, rows_per_group), _mod_spec(sc, tm, rows_per_group),
                  _mod_spec(gt, tm, rows_per_group), vec, vec,
                  row(dv_total), row(ob.shape[1]), row(oc.shape[1]),
                  res(w2), res(wa), res(wb), res(wc), res(wo)],
        out_specs=row(d),
        out_shape=jax.ShapeDtypeStruct((n, d), F32),
        compiler_params=_params("arbitrary"),
        name="merge",
    )(x, sh, sc, gt, g_pre, g_post, oa, ob, oc, w2, wa, wb, wc, wo)


def _dec_gate_kernel(pt_ref, *refs, pages_per_step, pages_per_block, n_steps):
    page_refs = refs[:pages_per_step]
    q_ref = refs[pages_per_step]
    idx_ref = refs[pages_per_step + 1]
    km_ref = refs[pages_per_step + 2]
    j = pl.program_id(1)
    blocks_per_step = pages_per_step // pages_per_block
    inv = float(N_HEADS) / (pages_per_block * page_refs[0].shape[0])
    per_head = [[] for _ in range(N_HEADS)]
    for blk in range(blocks_per_step):
        s = None
        for r in range(pages_per_block):
            part = _head_row_sums(page_refs[blk * pages_per_block + r][...])
            s = part if s is None else s + part
        for h in range(N_HEADS):
            per_head[h].append(s[h:h + 1] * inv)
    rows = pl.ds(pl.multiple_of(j * blocks_per_step, blocks_per_step), blocks_per_step)
    for h in range(N_HEADS):
        km_ref[h, rows, :] = jnp.concatenate(per_head[h], axis=0)

    @pl.when(j == n_steps - 1)
    def _():
        nb = km_ref.shape[1]
        tq = q_ref.shape[0]
        blk = lax.broadcasted_iota(jnp.int32, (tq, nb), 1)
        lane = lax.broadcasted_iota(jnp.int32, (tq, LANES), 1)
        out = jnp.zeros((tq, LANES), jnp.int32)
        for h in range(N_HEADS):
            hs = slice(h * LANES, (h + 1) * LANES)
            q_hi, q_lo = _split_hi_lo(q_ref[:, hs])
            k_hi, k_lo = _split_hi_lo(km_ref[h])
            g = _dot_nt(q_hi, k_hi) + _dot_nt(q_hi, k_lo) + _dot_nt(q_lo, k_hi)
            for r in range(MOBA_TOPK):
                m = jnp.max(g, axis=-1, keepdims=True)
                idx = jnp.min(jnp.where(g == m, blk, nb), axis=-1, keepdims=True)
                out = jnp.where(lane == h * 4 + r, idx, out)
                g = jnp.where(blk == idx, NEG_INF, g)
        idx_ref[...] = out


def _dec_gate(cache_k, layer, page_table, q_pad):
    _, _, page_rows, hd = cache_k.shape
    page = page_rows // N_HEADS
    w = N_HEADS * hd
    db, n_pages = page_table.shape
    pages_per_block = MOBA_BLOCK // page
    blocks_per_step = 8
    pages_per_step = blocks_per_step * pages_per_block
    assert n_pages % pages_per_step == 0
    n_steps = n_pages // pages_per_step
    nb = n_pages // pages_per_block
    assert nb >= MOBA_TOPK

    def page_spec(r):
        return pl.BlockSpec((None, None, page_rows, hd),
                            lambda b, j, pt: (layer, pt[b, j * pages_per_step + r], 0, 0))

    grid_spec = pltpu.PrefetchScalarGridSpec(
        num_scalar_prefetch=1,
        grid=(db, n_steps),
        in_specs=[page_spec(r) for r in range(pages_per_step)]
        + [pl.BlockSpec((None, q_pad.shape[1], w), lambda b, j, pt: (b, 0, 0))],
        out_specs=pl.BlockSpec((None, q_pad.shape[1], LANES), lambda b, j, pt: (b, 0, 0)),
        scratch_shapes=[pltpu.VMEM((N_HEADS, nb, hd), F32)],
    )
    return pl.pallas_call(
        functools.partial(_dec_gate_kernel, pages_per_step=pages_per_step,
                          pages_per_block=pages_per_block, n_steps=n_steps),
        grid_spec=grid_spec,
        out_shape=jax.ShapeDtypeStruct((db, q_pad.shape[1], LANES), jnp.int32),
        compiler_params=_params("arbitrary", "arbitrary"),
        name="moba_dec_gate",
    )(page_table, *([cache_k] * pages_per_step), q_pad)


def _dec_moba_kernel(idx_ref, pt_ref, *refs, n_tok, pages_per_block, scale):
    per_head = MOBA_TOPK * pages_per_block
    n_sel = N_HEADS * per_head
    k_refs = refs[:n_sel]
    v_refs = refs[n_sel:2 * n_sel]
    q_ref, kn_ref, vn_ref, o_ref = refs[2 * n_sel:]
    t = pl.program_id(1)
    tq = q_ref.shape[0]
    page = kn_ref.shape[0]
    hd = LANES
    row = lax.broadcasted_iota(jnp.int32, (tq, page), 0)
    col = lax.broadcasted_iota(jnp.int32, (tq, page), 1)
    own_ok = jnp.logical_and(col <= row, col < n_tok)
    this_row = lax.broadcasted_iota(jnp.int32, (tq, hd), 0) == t

    @pl.when(t == 0)
    def _():
        o_ref[...] = jnp.zeros_like(o_ref)

    for h in range(N_HEADS):
        hs = slice(h * hd, (h + 1) * hd)
        q = q_ref[:, hs].astype(BF16)
        head_rows = pl.ds(h, page, stride=N_HEADS)
        scores = [jnp.where(own_ok, _dot_nt(q, kn_ref[:, hs].astype(BF16)) * scale, NEG_INF)]
        values = [vn_ref[:, hs].astype(BF16)]
        for i in range(h * per_head, (h + 1) * per_head):
            scores.append(_dot_nt(q, k_refs[i][head_rows, :].astype(BF16)) * scale)
            values.append(v_refs[i][head_rows, :].astype(BF16))
        m = scores[0].max(axis=-1, keepdims=True)
        for s in scores[1:]:
            m = jnp.maximum(m, s.max(axis=-1, keepdims=True))
        l = jnp.zeros((tq, 1), F32)
        acc = jnp.zeros((tq, hd), F32)
        for s, v in zip(scores, values):
            p = jnp.exp(s - m)
            l = l + jnp.sum(p, axis=-1, keepdims=True)
            acc = acc + _dot(p.astype(BF16), v)
        o_ref[:, hs] = jnp.where(this_row, acc / l, o_ref[:, hs])


def _dec_moba(cache_k, cache_v, layer, page_table, sel_flat, q_pad, k_new_pad, v_new_pad, *, n_tok):
    _, _, page_rows, hd = cache_k.shape
    nh = N_HEADS
    page = page_rows // nh
    db = page_table.shape[0]
    pages_per_block = MOBA_BLOCK // page
    tq = q_pad.shape[1]

    def page_spec(h, r, half):
        def index(b, t, sel, pt):
            n = sel[((b * n_tok + t) * nh + h) * MOBA_TOPK + r]
            return (layer, pt[b, n * pages_per_block + half], 0, 0)
        return pl.BlockSpec((None, None, page_rows, hd), index)

    sel_specs = [page_spec(h, r, half) for h in range(nh) for r in range(MOBA_TOPK)
                 for half in range(pages_per_block)]
    per_b = lambda rows: pl.BlockSpec((None, rows, nh * hd), lambda b, t, sel, pt: (b, 0, 0))
    grid_spec = pltpu.PrefetchScalarGridSpec(
        num_scalar_prefetch=2,
        grid=(db, n_tok),
        in_specs=sel_specs + sel_specs + [per_b(tq), per_b(page), per_b(page)],
        out_specs=per_b(tq),
    )
    n_sel = len(sel_specs)
    return pl.pallas_call(
        functools.partial(_dec_moba_kernel, n_tok=n_tok, pages_per_block=pages_per_block,
                          scale=hd ** -0.5),
        grid_spec=grid_spec,
        out_shape=jax.ShapeDtypeStruct((db, tq, nh * hd), F32),
        compiler_params=_params("arbitrary", "arbitrary"),
        name="moba_dec_attend",
    )(sel_flat, page_table, *([cache_k] * n_sel), *([cache_v] * n_sel), q_pad, k_new_pad, v_new_pad)


def _dec_sb_page(get_k, get_v, qblk_ref, tri, c_ref, acc_ref, flag_ref, scale, mask, n_cols):
    hd = LANES
    z = None
    for h in range(N_HEADS):
        zh = _dot(get_k(h).astype(BF16), qblk_ref[h * hd:(h + 1) * hd, :])
        z = zh if z is None else z + zh
    z = z * scale
    sp = _softplus_neg_abs(z)
    log_beta = jnp.minimum(z, 0.0) - sp
    log_rem = -jnp.maximum(z, 0.0) - sp
    if mask is not None:
        log_rem = jnp.where(mask, log_rem, 0.0)
    hi, lo = _split_hi_lo(log_rem)
    incl = _dot(tri, hi) + _dot(tri, lo)
    c = c_ref[0:1, :]
    w = jnp.exp(log_beta + (incl - log_rem) + c)
    if mask is not None:
        w = jnp.where(mask, w, 0.0)
    w = w.astype(BF16)
    for h in range(N_HEADS):
        acc_ref[h] += _dot_tn(w, get_v(h).astype(BF16))
    c_new = c + incl[0:1, :]
    c_ref[0:1, :] = c_new
    lane = lax.broadcasted_iota(jnp.int32, c_new.shape, 1)
    live = jnp.max(jnp.where(lane < n_cols, c_new, NEG_INF)) > SB_EXP_ZERO
    flag_ref[0] = live.astype(jnp.int32)


def _dec_sb_kernel(pt_ref, *refs, pages_per_step, n_steps, n_tok, scale):
    k_refs = refs[:pages_per_step]
    v_refs = refs[pages_per_step:2 * pages_per_step]
    qblk_ref, kn_ref, vn_ref, o_ref, c_ref, acc_ref, flag_ref = refs[2 * pages_per_step:]
    j = pl.program_id(1)
    page = kn_ref.shape[0]
    hd = LANES
    n_cols = N_HEADS * n_tok
    r = lax.broadcasted_iota(jnp.int32, (page, page), 0)
    cc = lax.broadcasted_iota(jnp.int32, (page, page), 1)
    tri = (cc >= r).astype(BF16)

    @pl.when(j == 0)
    def _():
        c_ref[...] = jnp.zeros_like(c_ref)
        acc_ref[...] = jnp.zeros_like(acc_ref)
        key = lax.broadcasted_iota(jnp.int32, (page, LANES), 0)
        lane = lax.broadcasted_iota(jnp.int32, (page, LANES), 1)
        mask = jnp.logical_and(key < lane % n_tok, lane < n_cols)
        _dec_sb_page(lambda h: kn_ref[:, h * hd:(h + 1) * hd], lambda h: vn_ref[:, h * hd:(h + 1) * hd],
                     qblk_ref, tri, c_ref, acc_ref, flag_ref, scale, mask, n_cols)

    for p in range(pages_per_step):
        @pl.when(flag_ref[0] > 0)
        def _():
            _dec_sb_page(lambda h: k_refs[p][pl.ds(h, page, stride=N_HEADS), :],
                         lambda h: v_refs[p][pl.ds(h, page, stride=N_HEADS), :],
                         qblk_ref, tri, c_ref, acc_ref, flag_ref, scale, None, n_cols)

    @pl.when(j == n_steps - 1)
    def _():
        for h in range(N_HEADS):
            o_ref[:, h * hd:(h + 1) * hd] = acc_ref[h][0:o_ref.shape[0], :]


def _dec_sb(cache_k, cache_v, layer, page_table, qblk, k_new_pad, v_new_pad, *, n_tok):
    _, _, page_rows, hd = cache_k.shape
    page = page_rows // N_HEADS
    w = N_HEADS * hd
    db, n_pages = page_table.shape
    pages_per_step = 8
    assert n_pages % pages_per_step == 0 and page == LANES
    n_steps = n_pages // pages_per_step
    rows_out = 16
    assert N_HEADS * n_tok <= rows_out

    def page_spec(r):
        return pl.BlockSpec((None, None, page_rows, hd),
                            lambda b, j, pt: (layer, pt[b, n_pages - 1 - (j * pages_per_step + r)], 0, 0))

    per_b = lambda rows, cols: pl.BlockSpec((None, rows, cols), lambda b, j, pt: (b, 0, 0))
    grid_spec = pltpu.PrefetchScalarGridSpec(
        num_scalar_prefetch=1,
        grid=(db, n_steps),
        in_specs=[page_spec(r) for r in range(pages_per_step)] * 2
        + [per_b(w, LANES), per_b(page, w), per_b(page, w)],
        out_specs=per_b(rows_out, w),
        scratch_shapes=[pltpu.VMEM((8, LANES), F32), pltpu.VMEM((N_HEADS, LANES, hd), F32),
                        pltpu.SMEM((1,), jnp.int32)],
    )
    return pl.pallas_call(
        functools.partial(_dec_sb_kernel, pages_per_step=pages_per_step, n_steps=n_steps,
                          n_tok=n_tok, scale=(w // N_HEADS) ** -0.5),
        grid_spec=grid_spec,
        out_shape=jax.ShapeDtypeStruct((db, rows_out, w), F32),
        compiler_params=_params("arbitrary", "arbitrary"),
        name="sb_dec",
    )(page_table, *([cache_k] * pages_per_step), *([cache_v] * pages_per_step),
      qblk, k_new_pad, v_new_pad)


def _rope_tables(pos, hd):
    rope_dim = hd // 4
    half = rope_dim // 2
    inv = ROPE_THETA ** (-jnp.arange(half, dtype=F32) / half)
    ang = pos.astype(F32)[:, None] * inv[None, :]
    cos, sin = jnp.cos(ang), jnp.sin(ang)
    n = pos.shape[0]
    ones = jnp.ones((n, hd - rope_dim), F32)
    cos_t = jnp.concatenate([cos, cos, ones], axis=-1)
    sin_t = jnp.concatenate([-sin, sin, jnp.zeros_like(ones)], axis=-1)
    return cos_t, sin_t


def kernel(x_prompt, x_sample, cache_moba_k, cache_moba_v, cache_sb_k, cache_sb_v, state_gla, page_table, c_prompt, c_sample, w_ada, b_ada, g_ffn1_pre, g_ffn1_post, g_mix_pre, g_mix_post, g_ffn2_pre, g_ffn2_post, w_ffn1_in, w_ffn1_down, w_ffn2_in, w_ffn2_down, w_in, w_gla_alpha, b_gla_alpha, g_gla_head, w_br_gla, w_br_moba, w_br_sb, w_out):
    bsz, t_p, d = x_prompt.shape
    db, t_s, _ = x_sample.shape
    depth = w_ada.shape[0]
    _, n_phys, page, nh, hd = cache_moba_k.shape
    _, _, _, dk, dv = state_gla.shape
    rank = w_gla_alpha.shape[1]
    assert nh == N_HEADS
    dk_total, dv_total, hd_total = nh * dk, nh * dv, nh * hd
    past_len = page_table.shape[1] * page
    n_p, n_s = bsz * t_p, db * t_s
    tm_p = min(512, t_p)
    t_pad = 8
    assert t_s <= t_pad and t_p % tm_p == 0

    mod = _ada(jnp.concatenate([c_prompt, c_sample], axis=0), w_ada, b_ada)

    o_gr = 2 * dk_total + dv_total
    o_glr = o_gr + dv_total
    o_m = o_glr + rank
    o_s = o_m + 3 * hd_total
    o_g = o_s + 3 * hd_total

    cos_p, sin_p = _rope_tables(jnp.arange(t_p, dtype=jnp.int32), hd)
    cos_s, sin_s = _rope_tables(past_len + jnp.arange(t_s, dtype=jnp.int32), hd)
    cos_s, sin_s = jnp.tile(cos_s, (db, 1)), jnp.tile(sin_s, (db, 1))

    yp = x_prompt.reshape(n_p, d)
    ys = x_sample.reshape(n_s, d)
    cache_mk4 = cache_moba_k.reshape(depth, n_phys, page * nh, hd)
    cache_mv4 = cache_moba_v.reshape(depth, n_phys, page * nh, hd)
    cache_sk4 = cache_sb_k.reshape(depth, n_phys, page * nh, hd)
    cache_sv4 = cache_sb_v.reshape(depth, n_phys, page * nh, hd)

    outs_p = [[] for _ in range(5)]
    outs_s = [[] for _ in range(5)]
    for l in range(depth):
        vec = lambda a: a[l].reshape(1, -1)
        w1 = jnp.concatenate(
            [w_in[l][:, :o_gr], jnp.pad(w_in[l][:, o_glr:o_m], ((0, 0), (0, LANES - rank))),
             w_in[l][:, o_m:o_g]], axis=1).astype(BF16)
        w2 = jnp.concatenate([w_in[l][:, o_gr:o_glr], w_in[l][:, o_g:]], axis=1).astype(BF16)
        w_alpha = jnp.pad(w_gla_alpha[l], ((0, LANES - rank), (0, 0))).astype(BF16)
        wf1_in, wf1_dn = w_ffn1_in[l].astype(BF16), w_ffn1_down[l].astype(BF16)
        wf2_in, wf2_dn = w_ffn2_in[l].astype(BF16), w_ffn2_down[l].astype(BF16)
        wa, wb, wc = w_br_gla[l].astype(BF16), w_br_moba[l].astype(BF16), w_br_sb[l].astype(BF16)
        wo = w_out[l].astype(BF16)
        mods = jnp.split(mod[l], N_MOD, axis=-1)
        mod_p = [m[:bsz].reshape(bsz, 1, d) for m in mods]
        mod_s = [jnp.repeat(m[bsz:], t_s, axis=0).reshape(1, n_s, d) for m in mods]

        kw = dict(tm=tm_p, rows_per_group=t_p)
        yp = _ffn(yp, mod_p[0], mod_p[1], mod_p[2], vec(g_ffn1_pre), vec(g_ffn1_post), wf1_in, wf1_dn, **kw)
        (gq, gk, gv, la, mq, mk, mkb, mv, mvb, sq, sk, skb, sv, svb) = _mixin(
            yp, mod_p[3], mod_p[4], vec(g_mix_pre), cos_p, sin_p, w1, w_alpha, vec(b_gla_alpha),
            dk_total=dk_total, dv_total=dv_total, hd_total=hd_total, **kw)
        seq = lambda a: a.reshape(bsz, t_p, a.shape[-1])
        o_a, st = _gla(seq(gq), seq(gk), seq(la), seq(gv), jnp.zeros((bsz, nh, dk, dv), F32),
                       vec(g_gla_head), tt=tm_p, chunk=math.gcd(t_p, GLA_CHUNK))
        o_b = _moba(seq(mq), seq(mkb), seq(mvb), _kmean(mk, bsz))
        o_c = _sb(seq(sq), seq(skb), seq(svb))
        yp = _merge(yp, mod_p[3], mod_p[4], mod_p[5], vec(g_mix_pre), vec(g_mix_post),
                    o_a.reshape(n_p, dv_total), o_b.reshape(n_p, hd_total), o_c.reshape(n_p, hd_total),
                    w2, wa, wb, wc, wo, **kw)
        yp = _ffn(yp, mod_p[6], mod_p[7], mod_p[8], vec(g_ffn2_pre), vec(g_ffn2_post), wf2_in, wf2_dn, **kw)
        for lst, a in zip(outs_p, (mk, mv, sk, sv)):
            lst.append(a.reshape(bsz, t_p, nh, hd))
        outs_p[4].append(st)

        kw = dict(tm=n_s, rows_per_group=n_s)
        ys = _ffn(ys, mod_s[0], mod_s[1], mod_s[2], vec(g_ffn1_pre), vec(g_ffn1_post), wf1_in, wf1_dn, **kw)
        (gq, gk, gv, la, mq, mk, mkb, mv, mvb, sq, sk, skb, sv, svb) = _mixin(
            ys, mod_s[3], mod_s[4], vec(g_mix_pre), cos_s, sin_s, w1, w_alpha, vec(b_gla_alpha),
            dk_total=dk_total, dv_total=dv_total, hd_total=hd_total, **kw)
        seq = lambda a: a.reshape(db, t_s, -1)
        pad_t = lambda a, rows: jnp.pad(seq(a), ((0, 0), (0, rows - t_s), (0, 0)))
        o_a, st = _gla(pad_t(gq, t_pad), pad_t(gk, t_pad), pad_t(la, t_pad), pad_t(gv, t_pad),
                       state_gla[l], vec(g_gla_head), tt=t_pad, chunk=t_pad)
        o_a = o_a[:, :t_s]
        mq_pad = pad_t(mq.astype(F32), t_pad)
        sel = _dec_gate(cache_mk4, l, page_table, mq_pad)
        sel_flat = sel[:, :t_s, :nh * 4].reshape(db, t_s, nh, 4)[..., :MOBA_TOPK].reshape(-1)
        o_b = _dec_moba(cache_mk4, cache_mv4, l, page_table, sel_flat, mq_pad,
                        pad_t(mk, page), pad_t(mv, page), n_tok=t_s)[:, :t_s].astype(BF16)
        q4 = seq(sq).reshape(db, t_s, nh, hd)
        eye = jnp.eye(nh, dtype=BF16)
        qblk = jnp.einsum("bthd,hg->bhdgt", q4, eye).reshape(db, hd_total, nh * t_s)
        qblk = jnp.pad(qblk, ((0, 0), (0, 0), (0, LANES - nh * t_s)))
        o_raw = _dec_sb(cache_sk4, cache_sv4, l, page_table, qblk, pad_t(sk, page), pad_t(sv, page), n_tok=t_s)
        o_c = jnp.stack([o_raw[:, h * t_s:(h + 1) * t_s, h * hd:(h + 1) * hd] for h in range(nh)], axis=2)
        o_c = o_c.reshape(n_s, hd_total).astype(BF16)
        ys = _merge(ys, mod_s[3], mod_s[4], mod_s[5], vec(g_mix_pre), vec(g_mix_post),
                    o_a.reshape(n_s, dv_total), o_b.reshape(n_s, hd_total), o_c,
                    w2, wa, wb, wc, wo, **kw)
        ys = _ffn(ys, mod_s[6], mod_s[7], mod_s[8], vec(g_ffn2_pre), vec(g_ffn2_post), wf2_in, wf2_dn, **kw)
        for lst, a in zip(outs_s, (mk, mv, sk, sv)):
            lst.append(a.reshape(db, t_s, nh, hd))
        outs_s[4].append(st)

    stack = lambda lst: jnp.stack(lst)
    return (yp.reshape(bsz, t_p, d), ys.reshape(db, t_s, d),
            stack(outs_p[0]), stack(outs_p[1]), stack(outs_p[2]), stack(outs_p[3]), stack(outs_p[4]),
            stack(outs_s[0]), stack(outs_s[1]), stack(outs_s[2]), stack(outs_s[3]), stack(outs_s[4]))
```

```python
import functools
import math

import jax
import jax.numpy as jnp
from jax import lax
from jax.experimental import pallas as pl
from jax.experimental.pallas import tpu as pltpu

F32 = jnp.float32
BF16 = jnp.bfloat16

EPS = 1e-6
N_HEADS = 4
GLA_TAU = 16.0
GLA_CHUNK = 64
MOBA_BLOCK = 256
MOBA_TOPK = 3
ROPE_THETA = 500000.0
N_MOD = 9
LANES = 128
VMEM_LIMIT_BYTES = 56 * 1024 * 1024

NEG_INF = float("-inf")


def _dot(a, b):
    return jnp.dot(a, b, preferred_element_type=F32)


def _dot_nt(a, b):
    return lax.dot_general(a, b, (((1,), (1,)), ((), ())), preferred_element_type=F32)


def _dot_tn(a, b):
    return lax.dot_general(a, b, (((0,), (0,)), ((), ())), preferred_element_type=F32)


def _rms(x, g):
    return x * lax.rsqrt(jnp.mean(x * x, axis=-1, keepdims=True) + EPS) * g


def _silu(x):
    return x * jax.nn.sigmoid(x)


def _softplus_neg_abs(z):
    return jnp.log1p(jnp.exp(-jnp.abs(z)))


def _split_hi_lo(x):
    hi = x.astype(BF16)
    lo = (x - hi.astype(F32)).astype(BF16)
    return hi, lo


def _params(*sem):
    return pltpu.CompilerParams(dimension_semantics=sem, vmem_limit_bytes=VMEM_LIMIT_BYTES)


def _resident(shape, index_map):
    return pl.BlockSpec(shape, index_map, pipeline_mode=pl.Buffered(1))


def _mod_spec(mod, tm, rows_per_group):
    g, r, d = mod.shape
    if r == 1:
        tiles_per_group = rows_per_group // tm
        return pl.BlockSpec((None, 1, d), lambda i: (i // tiles_per_group, 0, 0))
    assert g == 1 and r == tm
    return pl.BlockSpec((None, r, d), lambda i: (0, 0, 0))


def _ada_kernel(c_ref, w_ref, b_ref, o_ref):
    a = _silu(c_ref[...]).astype(BF16)
    o_ref[...] = _dot(a, w_ref[...].astype(BF16)) + b_ref[...]


def _ada(c_all, w_ada, b_ada):
    depth, d, n = w_ada.shape
    m = c_all.shape[0]
    tn = 1024
    return pl.pallas_call(
        _ada_kernel,
        grid=(depth, n // tn),
        in_specs=[
            pl.BlockSpec((m, d), lambda l, j: (0, 0)),
            pl.BlockSpec((None, d, tn), lambda l, j: (l, 0, j)),
            pl.BlockSpec((None, 1, tn), lambda l, j: (l, 0, j)),
        ],
        out_specs=pl.BlockSpec((None, m, tn), lambda l, j: (l, 0, j)),
        out_shape=jax.ShapeDtypeStruct((depth, m, n), F32),
        compiler_params=_params("arbitrary", "arbitrary"),
        name="ada_mod",
    )(c_all, w_ada, b_ada.reshape(depth, 1, n))


def _ffn_kernel(x_ref, sh_ref, sc_ref, gt_ref, gpre_ref, gpost_ref, wg_ref, wu_ref, wd_ref,
                o_ref, a_ref, *, tf):
    x = x_ref[...]
    h = (_rms(x, gpre_ref[...]) * (1.0 + sc_ref[...]) + sh_ref[...]).astype(BF16)
    f = wd_ref.shape[0]
    for c in range(f // tf):
        sl = slice(c * tf, (c + 1) * tf)
        g = _dot(h, wg_ref[:, sl])
        u = _dot(h, wu_ref[:, sl])
        a_ref[:, sl] = (_silu(g) * u).astype(BF16)
    y = _dot(a_ref[...], wd_ref[...])
    o_ref[...] = x + 0.5 * gt_ref[...] * _rms(y, gpost_ref[...])


def _ffn(x, sh, sc, gt, g_pre, g_post, w_in, w_down, *, tm, rows_per_group):
    n, d = x.shape
    f = w_down.shape[0]
    tf = 256
    assert f % tf == 0 and n % tm == 0
    row = pl.BlockSpec((tm, d), lambda i: (i, 0))
    vec = pl.BlockSpec((1, d), lambda i: (0, 0))
    return pl.pallas_call(
        functools.partial(_ffn_kernel, tf=tf),
        grid=(n // tm,),
        in_specs=[row, _mod_spec(sh, tm, rows_per_group), _mod_spec(sc, tm, rows_per_group),
                  _mod_spec(gt, tm, rows_per_group), vec, vec,
                  _resident((d, f), lambda i: (0, 0)), _resident((d, f), lambda i: (0, 1)),
                  _resident((f, d), lambda i: (0, 0))],
        out_specs=row,
        out_shape=jax.ShapeDtypeStruct((n, d), F32),
        scratch_shapes=[pltpu.VMEM((tm, f), BF16)],
        compiler_params=_params("arbitrary"),
        name="ffn",
    )(x, sh, sc, gt, g_pre, g_post, w_in, w_in, w_down)


def _rope(x, cos_t, sin_t):
    quarter = cos_t.shape[-1] // 8
    lane = lax.broadcasted_iota(jnp.int32, cos_t.shape, 1)
    outs = []
    for h in range(x.shape[-1] // LANES):
        xh = x[:, h * LANES:(h + 1) * LANES]
        partner = jnp.where(lane < quarter, pltpu.roll(xh, LANES - quarter, 1), pltpu.roll(xh, quarter, 1))
        outs.append(xh * cos_t + partner * sin_t)
    return jnp.concatenate(outs, axis=-1)


def _store_token_head_rows(ref, x):
    tm = x.shape[0]
    for h in range(N_HEADS):
        ref[pl.ds(h, tm, stride=N_HEADS), :] = x[:, h * LANES:(h + 1) * LANES]


def _mixin_kernel(x_ref, sh_ref, sc_ref, g_ref, cos_ref, sin_ref, w_ref, wa_ref, ba_ref,
                  gq_ref, gk_ref, gv_ref, la_ref,
                  mq_ref, mk_ref, mkb_ref, mv_ref, mvb_ref,
                  sq_ref, sk_ref, skb_ref, sv_ref, svb_ref, *, dk_total, dv_total, hd_total):
    x = x_ref[...]
    h = (_rms(x, g_ref[...]) * (1.0 + sc_ref[...]) + sh_ref[...]).astype(BF16)
    o = 0
    qk = _dot(h, w_ref[:, o:o + 2 * dk_total]); o += 2 * dk_total
    dk = dk_total // N_HEADS
    gq_ref[...] = qk[:, :dk_total] * (dk ** -0.5)
    gk_ref[...] = qk[:, dk_total:]
    gv_ref[...] = _dot(h, w_ref[:, o:o + dv_total]); o += dv_total
    glr = _dot(h, w_ref[:, o:o + LANES]); o += LANES
    alpha = _dot(glr.astype(BF16), wa_ref[...]) + ba_ref[...]
    la_ref[...] = (jnp.minimum(alpha, 0.0) - _softplus_neg_abs(alpha)) / GLA_TAU
    cos_t = cos_ref[...]
    sin_t = sin_ref[...]
    mq = _rope(_dot(h, w_ref[:, o:o + hd_total]), cos_t, sin_t); o += hd_total
    mq_ref[...] = mq.astype(BF16)
    mk = _rope(_dot(h, w_ref[:, o:o + hd_total]), cos_t, sin_t); o += hd_total
    _store_token_head_rows(mk_ref, mk)
    mkb_ref[...] = mk.astype(BF16)
    mv = _dot(h, w_ref[:, o:o + hd_total]); o += hd_total
    _store_token_head_rows(mv_ref, mv)
    if mvb_ref.ndim == 3:
        for c in range(mvb_ref.shape[0]):
            mvb_ref[c] = mv[c * MOBA_BLOCK:(c + 1) * MOBA_BLOCK, :].T.astype(BF16)
    else:
        mvb_ref[...] = mv.astype(BF16)
    sq_ref[...] = _dot(h, w_ref[:, o:o + hd_total]).astype(BF16); o += hd_total
    sk = _dot(h, w_ref[:, o:o + hd_total]); o += hd_total
    _store_token_head_rows(sk_ref, sk)
    skb_ref[...] = sk.astype(BF16)
    sv = _dot(h, w_ref[:, o:o + hd_total]); o += hd_total
    _store_token_head_rows(sv_ref, sv)
    svb_ref[...] = sv.astype(BF16)


def _mixin(x, sh, sc, g_pre, cos_t, sin_t, w1, w_alpha, b_alpha, *, tm, rows_per_group,
           dk_total, dv_total, hd_total, v_transposed):
    n, d = x.shape
    width = w1.shape[1]
    table_tiles = cos_t.shape[0] // tm
    row = lambda w: pl.BlockSpec((tm, w), lambda i: (i, 0))
    vec = lambda w: pl.BlockSpec((1, w), lambda i: (0, 0))
    table = pl.BlockSpec((tm, LANES), lambda i: (i % table_tiles, 0))
    hd = hd_total // N_HEADS
    outs = [(1, dk_total, F32), (1, dk_total, F32), (1, dv_total, F32), (1, dk_total, F32),
            (1, hd_total, BF16), (N_HEADS, hd, F32), (1, hd_total, BF16), (N_HEADS, hd, F32), (1, hd_total, BF16),
            (1, hd_total, BF16), (N_HEADS, hd, F32), (1, hd_total, BF16), (N_HEADS, hd, F32), (1, hd_total, BF16)]
    out_specs = [pl.BlockSpec((tm * r, w), lambda i: (i, 0)) for r, w, _ in outs]
    out_shape = [jax.ShapeDtypeStruct((n * r, w), t) for r, w, t in outs]
    if v_transposed:
        assert tm % MOBA_BLOCK == 0
        kb = tm // MOBA_BLOCK
        out_specs[8] = pl.BlockSpec((kb, hd_total, MOBA_BLOCK), lambda i: (i, 0, 0))
        out_shape[8] = jax.ShapeDtypeStruct((n // MOBA_BLOCK, hd_total, MOBA_BLOCK), BF16)
    return pl.pallas_call(
        functools.partial(_mixin_kernel, dk_total=dk_total, dv_total=dv_total, hd_total=hd_total),
        grid=(n // tm,),
        in_specs=[row(d), _mod_spec(sh, tm, rows_per_group), _mod_spec(sc, tm, rows_per_group),
                  vec(d), table, table,
                  _resident((d, width), lambda i: (0, 0)),
                  _resident(w_alpha.shape, lambda i: (0, 0)), vec(dk_total)],
        out_specs=out_specs,
        out_shape=out_shape,
        compiler_params=_params("arbitrary"),
        name="mixin",
    )(x, sh, sc, g_pre, cos_t, sin_t, w1, w_alpha, b_alpha)


def _cumsum_rows(x):
    n = x.shape[0]
    row = lax.broadcasted_iota(jnp.int32, x.shape, 0)
    s = 1
    while s < n:
        x = x + jnp.where(row >= s, pltpu.roll(x, s, 0), 0.0)
        s *= 2
    return x


def _gla_kernel(q_ref, k_ref, la_ref, v_ref, s0_ref, g_ref, o_ref, sout_ref, st_ref, *, chunk):
    j = pl.program_id(1)
    n_pairs = N_HEADS // 2
    dk = s0_ref.shape[1]

    @pl.when(j == 0)
    def _():
        for p in range(n_pairs):
            pair = jnp.concatenate([s0_ref[2 * p], s0_ref[2 * p + 1]], axis=0)
            st_ref[p] = pair.T

    tt = q_ref.shape[0]
    lane = lax.broadcasted_iota(jnp.int32, (1, LANES), 1)
    first = lane < dk
    tril = (lax.broadcasted_iota(jnp.int32, (chunk, chunk), 1)
            <= lax.broadcasted_iota(jnp.int32, (chunk, chunk), 0))
    g = g_ref[...]

    def body(c, carry):
        rs = pl.ds(pl.multiple_of(c * chunk, chunk), chunk)
        b = _cumsum_rows(la_ref[rs, :])
        b_last = b[chunk - 1:chunk, :]
        q = q_ref[rs, :]
        k = k_ref[rs, :]
        qe = q * jnp.exp(b)
        ki = k * jnp.exp(-b)
        kd = k * jnp.exp(b_last - b)
        eb = jnp.exp(b_last)
        for p in range(n_pairs):
            ls = slice(p * LANES, (p + 1) * LANES)
            st = st_ref[p]
            st_b = st.astype(BF16)
            ki_b = ki[:, ls].astype(BF16)
            upd = jnp.zeros_like(st)
            for hl in range(2):
                hh = 2 * p + hl
                msk = first if hl == 0 else jnp.logical_not(first)
                qm = jnp.where(msk, qe[:, ls], 0.0).astype(BF16)
                kdm = jnp.where(msk, kd[:, ls], 0.0).astype(BF16)
                vs = slice(hh * LANES, (hh + 1) * LANES)
                vh = v_ref[rs, vs].astype(BF16)
                att = jnp.where(tril, _dot_nt(qm, ki_b), 0.0)
                o = _dot_nt(qm, st_b) + _dot(att.astype(BF16), vh)
                o_ref[rs, vs] = _rms(o, g)
                upd = upd + _dot_tn(vh, kdm)
            st_ref[p] = st * eb[:, ls] + upd
        return carry

    lax.fori_loop(0, tt // chunk, body, 0)

    @pl.when(j == pl.num_programs(1) - 1)
    def _():
        for p in range(n_pairs):
            t = st_ref[p].T
            sout_ref[2 * p] = t[:dk]
            sout_ref[2 * p + 1] = t[dk:]


def _gla(q, k, la, v, s0, g_head, *, tt, chunk):
    b, t, dkt = q.shape
    dvt = v.shape[-1]
    _, nh, dk, dv = s0.shape
    assert nh == N_HEADS and 2 * dk == LANES and dv == LANES and t % tt == 0 and tt % chunk == 0
    seq = lambda w: pl.BlockSpec((None, tt, w), lambda i, j: (i, j, 0))
    st = pl.BlockSpec((None, nh, dk, dv), lambda i, j: (i, 0, 0, 0))
    return pl.pallas_call(
        functools.partial(_gla_kernel, chunk=chunk),
        grid=(b, t // tt),
        in_specs=[seq(dkt), seq(dkt), seq(dkt), seq(dvt), st, pl.BlockSpec((1, dv), lambda i, j: (0, 0))],
        out_specs=[seq(dvt), st],
        out_shape=[jax.ShapeDtypeStruct((b, t, dvt), F32), jax.ShapeDtypeStruct(s0.shape, F32)],
        scratch_shapes=[pltpu.VMEM((nh // 2, LANES, LANES), F32)],
        compiler_params=_params("arbitrary", "arbitrary"),
        name="gla",
    )(q, k, la, v, s0, g_head)


def _head_row_sums(x):
    sub = 8
    part = jnp.sum(x.reshape(x.shape[0] // sub, sub, x.shape[1]), axis=0)
    out = part[0:N_HEADS]
    for g in range(1, sub // N_HEADS):
        out = out + part[g * N_HEADS:(g + 1) * N_HEADS]
    return out


def _kmean_kernel(k_ref, o_ref):
    o_ref[...] = _head_row_sums(k_ref[...]) * (float(N_HEADS) / k_ref.shape[0])


def _kmean(k_rows, bsz):
    hd = k_rows.shape[1]
    t = k_rows.shape[0] // (bsz * N_HEADS)
    nb = t // MOBA_BLOCK
    rows = MOBA_BLOCK * N_HEADS
    out = pl.pallas_call(
        _kmean_kernel,
        grid=(bsz, nb),
        in_specs=[pl.BlockSpec((rows, hd), lambda i, j: (i * nb + j, 0))],
        out_specs=pl.BlockSpec((None, None, N_HEADS, hd), lambda i, j: (i, j, 0, 0)),
        out_shape=jax.ShapeDtypeStruct((bsz, nb, N_HEADS, hd), F32),
        compiler_params=_params("arbitrary", "arbitrary"),
        name="moba_kmean",
    )(k_rows)
    return out.transpose(0, 2, 1, 3)


def _top_blocks_bias(gate, allowed, blk, n_blocks):
    g = jnp.where(allowed, gate, NEG_INF)
    sel = jnp.zeros(gate.shape, jnp.bool_)
    for _ in range(MOBA_TOPK):
        m = jnp.max(g, axis=0, keepdims=True)
        is_m = jnp.logical_and(g == m, allowed)
        idx = jnp.min(jnp.where(is_m, blk, n_blocks), axis=0, keepdims=True)
        pick = jnp.logical_and(blk == idx, is_m)
        sel = jnp.logical_or(sel, pick)
        g = jnp.where(pick, NEG_INF, g)
    return jnp.where(sel, 0.0, NEG_INF)


def _moba_kernel(q_ref, k_ref, vt_ref, km_ref, o_ref, m_ref, l_ref, acc_ref, bias_ref, *, scale):
    own = pl.program_id(1)
    tq = q_ref.shape[0]
    nb = km_ref.shape[1]
    hd = LANES
    heads = [slice(h * hd, (h + 1) * hd) for h in range(N_HEADS)]
    blk = lax.broadcasted_iota(jnp.int32, (nb, tq), 0)
    allowed = blk < own
    causal = (lax.broadcasted_iota(jnp.int32, (tq, tq), 0)
              <= lax.broadcasted_iota(jnp.int32, (tq, tq), 1))
    own_rows = pl.ds(pl.multiple_of(own * tq, tq), tq)
    for h, hs in enumerate(heads):
        qh = q_ref[:, hs]
        km_hi, km_lo = _split_hi_lo(km_ref[h])
        gate = _dot_nt(km_hi, qh) + _dot_nt(km_lo, qh)
        bias_ref[h] = _top_blocks_bias(gate, allowed, blk, nb)
        s = jnp.where(causal, _dot_nt(k_ref[own_rows, hs], qh) * scale, NEG_INF)
        m = jnp.max(s, axis=0, keepdims=True)
        p = jnp.exp(s - m)
        m_ref[h] = m
        l_ref[h] = jnp.sum(p, axis=0, keepdims=True)
        acc_ref[h] = _dot(vt_ref[own, hs, :], p.astype(BF16))

    def body(n, carry):
        rows = pl.ds(pl.multiple_of(n * tq, tq), tq)
        for h, hs in enumerate(heads):
            s = _dot_nt(k_ref[rows, hs], q_ref[:, hs]) * scale + bias_ref[h, pl.ds(n, 1), :]
            m = m_ref[h]
            m_new = jnp.maximum(m, jnp.max(s, axis=0, keepdims=True))
            a = jnp.exp(m - m_new)
            p = jnp.exp(s - m_new)
            m_ref[h] = m_new
            l_ref[h] = a * l_ref[h] + jnp.sum(p, axis=0, keepdims=True)
            acc_ref[h] = a * acc_ref[h] + _dot(vt_ref[n, hs, :], p.astype(BF16))
        return carry

    lax.fori_loop(0, own, body, 0)
    for h, hs in enumerate(heads):
        o_ref[:, hs] = (acc_ref[h] / l_ref[h]).T.astype(o_ref.dtype)


def _moba(q, k, vt, kmean):
    b, t, w = q.shape
    nb = kmean.shape[2]
    tq = MOBA_BLOCK
    hd = w // N_HEADS
    assert hd == LANES and t % tq == 0
    full = pl.BlockSpec((None, t, w), lambda i, j: (i, 0, 0))
    tile = pl.BlockSpec((None, tq, w), lambda i, j: (i, j, 0))
    return pl.pallas_call(
        functools.partial(_moba_kernel, scale=hd ** -0.5),
        grid=(b, t // tq),
        in_specs=[tile, full, pl.BlockSpec((None, t // tq, w, tq), lambda i, j: (i, 0, 0, 0)),
                  pl.BlockSpec((None, N_HEADS, nb, hd), lambda i, j: (i, 0, 0, 0))],
        out_specs=tile,
        out_shape=jax.ShapeDtypeStruct((b, t, w), BF16),
        scratch_shapes=[pltpu.VMEM((N_HEADS, 1, tq), F32), pltpu.VMEM((N_HEADS, 1, tq), F32),
                        pltpu.VMEM((N_HEADS, hd, tq), F32), pltpu.VMEM((N_HEADS, nb, tq), F32)],
        compiler_params=_params("arbitrary", "arbitrary"),
        name="moba",
    )(q, k, vt, kmean)


SB_EXP_ZERO = -104.0


def _sb_heads_block(get_q, get_k, get_v, tri_ext, c_ref, acc_ref, scale, mask):
    tk = tri_ext.shape[0]
    c_max = None
    for h in range(N_HEADS):
        z = _dot_nt(get_q(h), get_k(h)) * scale
        sp = _softplus_neg_abs(z)
        log_beta = jnp.minimum(z, 0.0) - sp
        log_rem = -jnp.maximum(z, 0.0) - sp
        if mask is not None:
            log_rem = jnp.where(mask, log_rem, 0.0)
        hi, lo = _split_hi_lo(log_rem)
        res = _dot(hi, tri_ext) + _dot(lo, tri_ext)
        incl = res[:, :tk]
        c_old = c_ref[h]
        w = jnp.exp(log_beta + (incl - log_rem) + jnp.tile(c_old, (1, tk // LANES)))
        if mask is not None:
            w = jnp.where(mask, w, 0.0)
        acc_ref[h] += _dot(w.astype(BF16), get_v(h))
        c_new = c_old + res[:, tk:]
        c_ref[h] = c_new
        m = jnp.max(c_new)
        c_max = m if c_max is None else jnp.maximum(c_max, m)
    return c_max


def _sb_kernel(q_ref, k_ref, v_ref, o_ref, acc_ref, c_ref, flag_ref, *, scale):
    i = pl.program_id(1)
    tq = q_ref.shape[0]
    hd = LANES
    r = lax.broadcasted_iota(jnp.int32, (tq, tq), 0)
    c = lax.broadcasted_iota(jnp.int32, (tq, tq), 1)
    tri_ext = jnp.concatenate([(r >= c).astype(BF16), jnp.ones((tq, LANES), BF16)], axis=1)
    strict = c < r
    get_q = lambda h: q_ref[:, h * hd:(h + 1) * hd]

    def block(n, mask):
        rows = pl.ds(pl.multiple_of(n * tq, tq), tq)
        c_max = _sb_heads_block(get_q, lambda h: k_ref[rows, h * hd:(h + 1) * hd],
                                lambda h: v_ref[rows, h * hd:(h + 1) * hd],
                                tri_ext, c_ref, acc_ref, scale, mask)
        flag_ref[0] = (c_max > SB_EXP_ZERO).astype(jnp.int32)

    acc_ref[...] = jnp.zeros_like(acc_ref)
    c_ref[...] = jnp.zeros_like(c_ref)
    block(i, strict)

    def body(t, carry):
        @pl.when(flag_ref[0] > 0)
        def _():
            block(i - 1 - t, None)
        return carry

    lax.fori_loop(0, i, body, 0)
    for h in range(N_HEADS):
        o_ref[:, h * hd:(h + 1) * hd] = acc_ref[h].astype(o_ref.dtype)


def _sb(q, k, v):
    b, t, w = q.shape
    tq = min(256, t)
    hd = w // N_HEADS
    assert hd == LANES and t % tq == 0 and tq % LANES == 0
    full = pl.BlockSpec((None, t, w), lambda i, j: (i, 0, 0))
    tile = pl.BlockSpec((None, tq, w), lambda i, j: (i, j, 0))
    return pl.pallas_call(
        functools.partial(_sb_kernel, scale=hd ** -0.5),
        grid=(b, t // tq),
        in_specs=[tile, full, full],
        out_specs=tile,
        out_shape=jax.ShapeDtypeStruct((b, t, w), BF16),
        scratch_shapes=[pltpu.VMEM((N_HEADS, tq, hd), F32), pltpu.VMEM((N_HEADS, tq, LANES), F32),
                        pltpu.SMEM((1,), jnp.int32)],
        compiler_params=_params("arbitrary", "arbitrary"),
        name="stickbreak",
    )(q, k, v)


def _merge_kernel(x_ref, sh_ref, sc_ref, gt_ref, gpre_ref, gpost_ref, oa_ref, ob_ref, oc_ref,
                  w2_ref, wa_ref, wb_ref, wc_ref, wo_ref, o_ref, *, dv_total):
    x = x_ref[...]
    d = x.shape[-1]
    h = (_rms(x, gpre_ref[...]) * (1.0 + sc_ref[...]) + sh_ref[...]).astype(BF16)
    gr = _dot(h, w2_ref[:, :dv_total])
    oa = (oa_ref[...] * _silu(gr)).astype(BF16)
    merged = jax.nn.sigmoid(_dot(h, w2_ref[:, dv_total:dv_total + d])) * _dot(oa, wa_ref[...])
    merged = merged + (jax.nn.sigmoid(_dot(h, w2_ref[:, dv_total + d:dv_total + 2 * d]))
                       * _dot(ob_ref[...], wb_ref[...]))
    merged = merged + (jax.nn.sigmoid(_dot(h, w2_ref[:, dv_total + 2 * d:dv_total + 3 * d]))
                       * _dot(oc_ref[...], wc_ref[...]))
    m = _dot(merged.astype(BF16), wo_ref[...])
    o_ref[...] = x + gt_ref[...] * _rms(m, gpost_ref[...])


def _merge(x, sh, sc, gt, g_pre, g_post, oa, ob, oc, w2, wa, wb, wc, wo, *, tm, rows_per_group):
    n, d = x.shape
    dv_total = oa.shape[1]
    row = lambda w: pl.BlockSpec((tm, w), lambda i: (i, 0))
    vec = pl.BlockSpec((1, d), lambda i: (0, 0))
    res = lambda a: _resident(a.shape, lambda i: (0, 0))
    return pl.pallas_call(
        functools.partial(_merge_kernel, dv_total=dv_total),
        grid=(n // tm,),
        in_specs=[row(d), _mod_spec(sh, tm, rows_per_group), _mod_spec(sc, tm, rows_per_group),
                  _mod_spec(gt, tm, rows_per_group), vec, vec,
                  row(dv_total), row(ob.shape[1]), row(oc.shape[1]),
                  res(w2), res(wa), res(wb), res(wc), res(wo)],
        out_specs=row(d),
        out_shape=jax.ShapeDtypeStruct((n, d), F32),
        compiler_params=_params("arbitrary"),
        name="merge",
    )(x, sh, sc, gt, g_pre, g_post, oa, ob, oc, w2, wa, wb, wc, wo)


def _dec_gate_kernel(pt_ref, *refs, pages_per_step, pages_per_block, n_steps):
    page_refs = refs[:pages_per_step]
    q_ref = refs[pages_per_step]
    idx_ref = refs[pages_per_step + 1]
    km_ref = refs[pages_per_step + 2]
    j = pl.program_id(1)
    blocks_per_step = pages_per_step // pages_per_block
    inv = float(N_HEADS) / (pages_per_block * page_refs[0].shape[0])
    per_head = [[] for _ in range(N_HEADS)]
    for blk in range(blocks_per_step):
        s = None
        for r in range(pages_per_block):
            part = _head_row_sums(page_refs[blk * pages_per_block + r][...])
            s = part if s is None else s + part
        for h in range(N_HEADS):
            per_head[h].append(s[h:h + 1] * inv)
    rows = pl.ds(pl.multiple_of(j * blocks_per_step, blocks_per_step), blocks_per_step)
    for h in range(N_HEADS):
        km_ref[h, rows, :] = jnp.concatenate(per_head[h], axis=0)

    @pl.when(j == n_steps - 1)
    def _():
        nb = km_ref.shape[1]
        tq = q_ref.shape[0]
        blk = lax.broadcasted_iota(jnp.int32, (tq, nb), 1)
        lane = lax.broadcasted_iota(jnp.int32, (tq, LANES), 1)
        out = jnp.zeros((tq, LANES), jnp.int32)
        for h in range(N_HEADS):
            hs = slice(h * LANES, (h + 1) * LANES)
            q_hi, q_lo = _split_hi_lo(q_ref[:, hs])
            k_hi, k_lo = _split_hi_lo(km_ref[h])
            g = _dot_nt(q_hi, k_hi) + _dot_nt(q_hi, k_lo) + _dot_nt(q_lo, k_hi)
            for r in range(MOBA_TOPK):
                m = jnp.max(g, axis=-1, keepdims=True)
                idx = jnp.min(jnp.where(g == m, blk, nb), axis=-1, keepdims=True)
                out = jnp.where(lane == h * 4 + r, idx, out)
                g = jnp.where(blk == idx, NEG_INF, g)
        idx_ref[...] = out


def _dec_gate(cache_k, layer, page_table, q_pad):
    _, _, page_rows, hd = cache_k.shape
    page = page_rows // N_HEADS
    w = N_HEADS * hd
    db, n_pages = page_table.shape
    pages_per_block = MOBA_BLOCK // page
    blocks_per_step = 8
    pages_per_step = blocks_per_step * pages_per_block
    assert n_pages % pages_per_step == 0
    n_steps = n_pages // pages_per_step
    nb = n_pages // pages_per_block
    assert nb >= MOBA_TOPK

    def page_spec(r):
        return pl.BlockSpec((None, None, page_rows, hd),
                            lambda b, j, pt: (layer, pt[b, j * pages_per_step + r], 0, 0))

    grid_spec = pltpu.PrefetchScalarGridSpec(
        num_scalar_prefetch=1,
        grid=(db, n_steps),
        in_specs=[page_spec(r) for r in range(pages_per_step)]
        + [pl.BlockSpec((None, q_pad.shape[1], w), lambda b, j, pt: (b, 0, 0))],
        out_specs=pl.BlockSpec((None, q_pad.shape[1], LANES), lambda b, j, pt: (b, 0, 0)),
        scratch_shapes=[pltpu.VMEM((N_HEADS, nb, hd), F32)],
    )
    return pl.pallas_call(
        functools.partial(_dec_gate_kernel, pages_per_step=pages_per_step,
                          pages_per_block=pages_per_block, n_steps=n_steps),
        grid_spec=grid_spec,
        out_shape=jax.ShapeDtypeStruct((db, q_pad.shape[1], LANES), jnp.int32),
        compiler_params=_params("arbitrary", "arbitrary"),
        name="moba_dec_gate",
    )(page_table, *([cache_k] * pages_per_step), q_pad)


def _dec_moba_kernel(idx_ref, pt_ref, *refs, n_tok, pages_per_block, scale):
    per_head = MOBA_TOPK * pages_per_block
    n_sel = N_HEADS * per_head
    k_refs = refs[:n_sel]
    v_refs = refs[n_sel:2 * n_sel]
    q_ref, kn_ref, vn_ref, o_ref = refs[2 * n_sel:]
    t = pl.program_id(1)
    tq = q_ref.shape[0]
    page = kn_ref.shape[0]
    hd = LANES
    row = lax.broadcasted_iota(jnp.int32, (tq, page), 0)
    col = lax.broadcasted_iota(jnp.int32, (tq, page), 1)
    own_ok = jnp.logical_and(col <= row, col < n_tok)
    this_row = lax.broadcasted_iota(jnp.int32, (tq, hd), 0) == t

    @pl.when(t == 0)
    def _():
        o_ref[...] = jnp.zeros_like(o_ref)

    for h in range(N_HEADS):
        hs = slice(h * hd, (h + 1) * hd)
        q = q_ref[:, hs].astype(BF16)
        head_rows = pl.ds(h, page, stride=N_HEADS)
        scores = [jnp.where(own_ok, _dot_nt(q, kn_ref[:, hs].astype(BF16)) * scale, NEG_INF)]
        values = [vn_ref[:, hs].astype(BF16)]
        for i in range(h * per_head, (h + 1) * per_head):
            scores.append(_dot_nt(q, k_refs[i][head_rows, :].astype(BF16)) * scale)
            values.append(v_refs[i][head_rows, :].astype(BF16))
        m = scores[0].max(axis=-1, keepdims=True)
        for s in scores[1:]:
            m = jnp.maximum(m, s.max(axis=-1, keepdims=True))
        l = jnp.zeros((tq, 1), F32)
        acc = jnp.zeros((tq, hd), F32)
        for s, v in zip(scores, values):
            p = jnp.exp(s - m)
            l = l + jnp.sum(p, axis=-1, keepdims=True)
            acc = acc + _dot(p.astype(BF16), v)
        o_ref[:, hs] = jnp.where(this_row, acc / l, o_ref[:, hs])


def _dec_moba(cache_k, cache_v, layer, page_table, sel_flat, q_pad, k_new_pad, v_new_pad, *, n_tok):
    _, _, page_rows, hd = cache_k.shape
    nh = N_HEADS
    page = page_rows // nh
    db = page_table.shape[0]
    pages_per_block = MOBA_BLOCK // page
    tq = q_pad.shape[1]

    def page_spec(h, r, half):
        def index(b, t, sel, pt):
            n = sel[((b * n_tok + t) * nh + h) * MOBA_TOPK + r]
            return (layer, pt[b, n * pages_per_block + half], 0, 0)
        return pl.BlockSpec((None, None, page_rows, hd), index)

    sel_specs = [page_spec(h, r, half) for h in range(nh) for r in range(MOBA_TOPK)
                 for half in range(pages_per_block)]
    per_b = lambda rows: pl.BlockSpec((None, rows, nh * hd), lambda b, t, sel, pt: (b, 0, 0))
    grid_spec = pltpu.PrefetchScalarGridSpec(
        num_scalar_prefetch=2,
        grid=(db, n_tok),
        in_specs=sel_specs + sel_specs + [per_b(tq), per_b(page), per_b(page)],
        out_specs=per_b(tq),
    )
    n_sel = len(sel_specs)
    return pl.pallas_call(
        functools.partial(_dec_moba_kernel, n_tok=n_tok, pages_per_block=pages_per_block,
                          scale=hd ** -0.5),
        grid_spec=grid_spec,
        out_shape=jax.ShapeDtypeStruct((db, tq, nh * hd), F32),
        compiler_params=_params("arbitrary", "arbitrary"),
        name="moba_dec_attend",
    )(sel_flat, page_table, *([cache_k] * n_sel), *([cache_v] * n_sel), q_pad, k_new_pad, v_new_pad)


def _dec_sb_page(get_k, get_v, qblk_ref, tri, c_ref, acc_ref, flag_ref, scale, mask, n_cols):
    hd = LANES
    z = None
    for h in range(N_HEADS):
        zh = _dot(get_k(h).astype(BF16), qblk_ref[h * hd:(h + 1) * hd, :])
        z = zh if z is None else z + zh
    z = z * scale
    sp = _softplus_neg_abs(z)
    log_beta = jnp.minimum(z, 0.0) - sp
    log_rem = -jnp.maximum(z, 0.0) - sp
    if mask is not None:
        log_rem = jnp.where(mask, log_rem, 0.0)
    hi, lo = _split_hi_lo(log_rem)
    incl = _dot(tri, hi) + _dot(tri, lo)
    c = c_ref[0:1, :]
    w = jnp.exp(log_beta + (incl - log_rem) + c)
    if mask is not None:
        w = jnp.where(mask, w, 0.0)
    w = w.astype(BF16)
    for h in range(N_HEADS):
        acc_ref[h] += _dot_tn(w, get_v(h).astype(BF16))
    c_new = c + incl[0:1, :]
    c_ref[0:1, :] = c_new
    lane = lax.broadcasted_iota(jnp.int32, c_new.shape, 1)
    live = jnp.max(jnp.where(lane < n_cols, c_new, NEG_INF)) > SB_EXP_ZERO
    flag_ref[0] = live.astype(jnp.int32)


def _dec_sb_kernel(pt_ref, *refs, pages_per_step, n_steps, n_tok, scale):
    k_refs = refs[:pages_per_step]
    v_refs = refs[pages_per_step:2 * pages_per_step]
    qblk_ref, kn_ref, vn_ref, o_ref, c_ref, acc_ref, flag_ref = refs[2 * pages_per_step:]
    j = pl.program_id(1)
    page = kn_ref.shape[0]
    hd = LANES
    n_cols = N_HEADS * n_tok
    r = lax.broadcasted_iota(jnp.int32, (page, page), 0)
    cc = lax.broadcasted_iota(jnp.int32, (page, page), 1)
    tri = (cc >= r).astype(BF16)

    @pl.when(j == 0)
    def _():
        c_ref[...] = jnp.zeros_like(c_ref)
        acc_ref[...] = jnp.zeros_like(acc_ref)
        key = lax.broadcasted_iota(jnp.int32, (page, LANES), 0)
        lane = lax.broadcasted_iota(jnp.int32, (page, LANES), 1)
        mask = jnp.logical_and(key < lane % n_tok, lane < n_cols)
        _dec_sb_page(lambda h: kn_ref[:, h * hd:(h + 1) * hd], lambda h: vn_ref[:, h * hd:(h + 1) * hd],
                     qblk_ref, tri, c_ref, acc_ref, flag_ref, scale, mask, n_cols)

    for p in range(pages_per_step):
        @pl.when(flag_ref[0] > 0)
        def _():
            _dec_sb_page(lambda h: k_refs[p][pl.ds(h, page, stride=N_HEADS), :],
                         lambda h: v_refs[p][pl.ds(h, page, stride=N_HEADS), :],
                         qblk_ref, tri, c_ref, acc_ref, flag_ref, scale, None, n_cols)

    @pl.when(j == n_steps - 1)
    def _():
        for h in range(N_HEADS):
            o_ref[:, h * hd:(h + 1) * hd] = acc_ref[h][0:o_ref.shape[0], :]


def _dec_sb(cache_k, cache_v, layer, page_table, qblk, k_new_pad, v_new_pad, *, n_tok):
    _, _, page_rows, hd = cache_k.shape
    page = page_rows // N_HEADS
    w = N_HEADS * hd
    db, n_pages = page_table.shape
    pages_per_step = 8
    assert n_pages % pages_per_step == 0 and page == LANES
    n_steps = n_pages // pages_per_step
    rows_out = 16
    assert N_HEADS * n_tok <= rows_out

    def page_spec(r):
        return pl.BlockSpec((None, None, page_rows, hd),
                            lambda b, j, pt: (layer, pt[b, n_pages - 1 - (j * pages_per_step + r)], 0, 0))

    per_b = lambda rows, cols: pl.BlockSpec((None, rows, cols), lambda b, j, pt: (b, 0, 0))
    grid_spec = pltpu.PrefetchScalarGridSpec(
        num_scalar_prefetch=1,
        grid=(db, n_steps),
        in_specs=[page_spec(r) for r in range(pages_per_step)] * 2
        + [per_b(w, LANES), per_b(page, w), per_b(page, w)],
        out_specs=per_b(rows_out, w),
        scratch_shapes=[pltpu.VMEM((8, LANES), F32), pltpu.VMEM((N_HEADS, LANES, hd), F32),
                        pltpu.SMEM((1,), jnp.int32)],
    )
    return pl.pallas_call(
        functools.partial(_dec_sb_kernel, pages_per_step=pages_per_step, n_steps=n_steps,
                          n_tok=n_tok, scale=(w // N_HEADS) ** -0.5),
        grid_spec=grid_spec,
        out_shape=jax.ShapeDtypeStruct((db, rows_out, w), F32),
        compiler_params=_params("arbitrary", "arbitrary"),
        name="sb_dec",
    )(page_table, *([cache_k] * pages_per_step), *([cache_v] * pages_per_step),
      qblk, k_new_pad, v_new_pad)


def _rope_tables(pos, hd):
    rope_dim = hd // 4
    half = rope_dim // 2
    inv = ROPE_THETA ** (-jnp.arange(half, dtype=F32) / half)
    ang = pos.astype(F32)[:, None] * inv[None, :]
    cos, sin = jnp.cos(ang), jnp.sin(ang)
    n = pos.shape[0]
    ones = jnp.ones((n, hd - rope_dim), F32)
    cos_t = jnp.concatenate([cos, cos, ones], axis=-1)
    sin_t = jnp.concatenate([-sin, sin, jnp.zeros_like(ones)], axis=-1)
    return cos_t, sin_t


def kernel(x_prompt, x_sample, cache_moba_k, cache_moba_v, cache_sb_k, cache_sb_v, state_gla, page_table, c_prompt, c_sample, w_ada, b_ada, g_ffn1_pre, g_ffn1_post, g_mix_pre, g_mix_post, g_ffn2_pre, g_ffn2_post, w_ffn1_in, w_ffn1_down, w_ffn2_in, w_ffn2_down, w_in, w_gla_alpha, b_gla_alpha, g_gla_head, w_br_gla, w_br_moba, w_br_sb, w_out):
    bsz, t_p, d = x_prompt.shape
    db, t_s, _ = x_sample.shape
    depth = w_ada.shape[0]
    _, n_phys, page, nh, hd = cache_moba_k.shape
    _, _, _, dk, dv = state_gla.shape
    rank = w_gla_alpha.shape[1]
    assert nh == N_HEADS
    dk_total, dv_total, hd_total = nh * dk, nh * dv, nh * hd
    past_len = page_table.shape[1] * page
    n_p, n_s = bsz * t_p, db * t_s
    tm_p = min(512, t_p)
    t_pad = 8
    assert t_s <= t_pad and t_p % tm_p == 0

    mod = _ada(jnp.concatenate([c_prompt, c_sample], axis=0), w_ada, b_ada)

    o_gr = 2 * dk_total + dv_total
    o_glr = o_gr + dv_total
    o_m = o_glr + rank
    o_s = o_m + 3 * hd_total
    o_g = o_s + 3 * hd_total

    cos_p, sin_p = _rope_tables(jnp.arange(t_p, dtype=jnp.int32), hd)
    cos_s, sin_s = _rope_tables(past_len + jnp.arange(t_s, dtype=jnp.int32), hd)
    cos_s, sin_s = jnp.tile(cos_s, (db, 1)), jnp.tile(sin_s, (db, 1))

    yp = x_prompt.reshape(n_p, d)
    ys = x_sample.reshape(n_s, d)
    cache_mk4 = cache_moba_k.reshape(depth, n_phys, page * nh, hd)
    cache_mv4 = cache_moba_v.reshape(depth, n_phys, page * nh, hd)
    cache_sk4 = cache_sb_k.reshape(depth, n_phys, page * nh, hd)
    cache_sv4 = cache_sb_v.reshape(depth, n_phys, page * nh, hd)

    outs_p = [[] for _ in range(5)]
    outs_s = [[] for _ in range(5)]
    for l in range(depth):
        vec = lambda a: a[l].reshape(1, -1)
        w1 = jnp.concatenate(
            [w_in[l][:, :o_gr], jnp.pad(w_in[l][:, o_glr:o_m], ((0, 0), (0, LANES - rank))),
             w_in[l][:, o_m:o_g]], axis=1).astype(BF16)
        w2 = jnp.concatenate([w_in[l][:, o_gr:o_glr], w_in[l][:, o_g:]], axis=1).astype(BF16)
        w_alpha = jnp.pad(w_gla_alpha[l], ((0, LANES - rank), (0, 0))).astype(BF16)
        wf1_in, wf1_dn = w_ffn1_in[l].astype(BF16), w_ffn1_down[l].astype(BF16)
        wf2_in, wf2_dn = w_ffn2_in[l].astype(BF16), w_ffn2_down[l].astype(BF16)
        wa, wb, wc = w_br_gla[l].astype(BF16), w_br_moba[l].astype(BF16), w_br_sb[l].astype(BF16)
        wo = w_out[l].astype(BF16)
        mods = jnp.split(mod[l], N_MOD, axis=-1)
        mod_p = [m[:bsz].reshape(bsz, 1, d) for m in mods]
        mod_s = [jnp.repeat(m[bsz:], t_s, axis=0).reshape(1, n_s, d) for m in mods]

        kw = dict(tm=tm_p, rows_per_group=t_p)
        yp = _ffn(yp, mod_p[0], mod_p[1], mod_p[2], vec(g_ffn1_pre), vec(g_ffn1_post), wf1_in, wf1_dn, **kw)
        (gq, gk, gv, la, mq, mk, mkb, mv, mvb, sq, sk, skb, sv, svb) = _mixin(
            yp, mod_p[3], mod_p[4], vec(g_mix_pre), cos_p, sin_p, w1, w_alpha, vec(b_gla_alpha),
            dk_total=dk_total, dv_total=dv_total, hd_total=hd_total, v_transposed=True, **kw)
        seq = lambda a: a.reshape(bsz, t_p, a.shape[-1])
        o_a, st = _gla(seq(gq), seq(gk), seq(la), seq(gv), jnp.zeros((bsz, nh, dk, dv), F32),
                       vec(g_gla_head), tt=tm_p, chunk=math.gcd(t_p, GLA_CHUNK))
        o_b = _moba(seq(mq), seq(mkb), mvb.reshape(bsz, t_p // MOBA_BLOCK, hd_total, MOBA_BLOCK),
                    _kmean(mk, bsz))
        o_c = _sb(seq(sq), seq(skb), seq(svb))
        yp = _merge(yp, mod_p[3], mod_p[4], mod_p[5], vec(g_mix_pre), vec(g_mix_post),
                    o_a.reshape(n_p, dv_total), o_b.reshape(n_p, hd_total), o_c.reshape(n_p, hd_total),
                    w2, wa, wb, wc, wo, **kw)
        yp = _ffn(yp, mod_p[6], mod_p[7], mod_p[8], vec(g_ffn2_pre), vec(g_ffn2_post), wf2_in, wf2_dn, **kw)
        for lst, a in zip(outs_p, (mk, mv, sk, sv)):
            lst.append(a.reshape(bsz, t_p, nh, hd))
        outs_p[4].append(st)

        kw = dict(tm=n_s, rows_per_group=n_s)
        ys = _ffn(ys, mod_s[0], mod_s[1], mod_s[2], vec(g_ffn1_pre), vec(g_ffn1_post), wf1_in, wf1_dn, **kw)
        (gq, gk, gv, la, mq, mk, mkb, mv, mvb, sq, sk, skb, sv, svb) = _mixin(
            ys, mod_s[3], mod_s[4], vec(g_mix_pre), cos_s, sin_s, w1, w_alpha, vec(b_gla_alpha),
            dk_total=dk_total, dv_total=dv_total, hd_total=hd_total, v_transposed=False, **kw)
        seq = lambda a: a.reshape(db, t_s, -1)
        pad_t = lambda a, rows: jnp.pad(seq(a), ((0, 0), (0, rows - t_s), (0, 0)))
        o_a, st = _gla(pad_t(gq, t_pad), pad_t(gk, t_pad), pad_t(la, t_pad), pad_t(gv, t_pad),
                       state_gla[l], vec(g_gla_head), tt=t_pad, chunk=t_pad)
        o_a = o_a[:, :t_s]
        mq_pad = pad_t(mq.astype(F32), t_pad)
        sel = _dec_gate(cache_mk4, l, page_table, mq_pad)
        sel_flat = sel[:, :t_s, :nh * 4].reshape(db, t_s, nh, 4)[..., :MOBA_TOPK].reshape(-1)
        o_b = _dec_moba(cache_mk4, cache_mv4, l, page_table, sel_flat, mq_pad,
                        pad_t(mk, page), pad_t(mv, page), n_tok=t_s)[:, :t_s].astype(BF16)
        q4 = seq(sq).reshape(db, t_s, nh, hd)
        eye = jnp.eye(nh, dtype=BF16)
        qblk = jnp.einsum("bthd,hg->bhdgt", q4, eye).reshape(db, hd_total, nh * t_s)
        qblk = jnp.pad(qblk, ((0, 0), (0, 0), (0, LANES - nh * t_s)))
        o_raw = _dec_sb(cache_sk4, cache_sv4, l, page_table, qblk, pad_t(sk, page), pad_t(sv, page), n_tok=t_s)
        o_c = jnp.stack([o_raw[:, h * t_s:(h + 1) * t_s, h * hd:(h + 1) * hd] for h in range(nh)], axis=2)
        o_c = o_c.reshape(n_s, hd_total).astype(BF16)
        ys = _merge(ys, mod_s[3], mod_s[4], mod_s[5], vec(g_mix_pre), vec(g_mix_post),
                    o_a.reshape(n_s, dv_total), o_b.reshape(n_s, hd_total), o_c,
                    w2, wa, wb, wc, wo, **kw)
        ys = _ffn(ys, mod_s[6], mod_s[7], mod_s[8], vec(g_ffn2_pre), vec(g_ffn2_post), wf2_in, wf2_dn, **kw)
        for lst, a in zip(outs_s, (mk, mv, sk, sv)):
            lst.append(a.reshape(db, t_s, nh, hd))
        outs_s[4].append(st)

    stack = lambda lst: jnp.stack(lst)
    return (yp.reshape(bsz, t_p, d), ys.reshape(db, t_s, d),
            stack(outs_p[0]), stack(outs_p[1]), stack(outs_p[2]), stack(outs_p[3]), stack(outs_p[4]),
            stack(outs_s[0]), stack(outs_s[1]), stack(outs_s[2]), stack(outs_s[3]), stack(outs_s[4]))
```

```python
import functools
import math

import jax
import jax.numpy as jnp
from jax import lax
from jax.experimental import pallas as pl
from jax.experimental.pallas import tpu as pltpu

F32 = jnp.float32
BF16 = jnp.bfloat16

EPS = 1e-6
N_HEADS = 4
GLA_TAU = 16.0
GLA_CHUNK = 64
MOBA_BLOCK = 256
MOBA_TOPK = 3
ROPE_THETA = 500000.0
N_MOD = 9
LANES = 128
VMEM_LIMIT_BYTES = 56 * 1024 * 1024

NEG_INF = float("-inf")


def _dot(a, b):
    return jnp.dot(a, b, preferred_element_type=F32)


def _dot_nt(a, b):
    return lax.dot_general(a, b, (((1,), (1,)), ((), ())), preferred_element_type=F32)


def _dot_tn(a, b):
    return lax.dot_general(a, b, (((0,), (0,)), ((), ())), preferred_element_type=F32)


def _rms(x, g):
    return x * lax.rsqrt(jnp.mean(x * x, axis=-1, keepdims=True) + EPS) * g


def _silu(x):
    return x * jax.nn.sigmoid(x)


def _softplus_neg_abs(z):
    return jnp.log1p(jnp.exp(-jnp.abs(z)))


def _split_hi_lo(x):
    hi = x.astype(BF16)
    lo = (x - hi.astype(F32)).astype(BF16)
    return hi, lo


def _params(*sem):
    return pltpu.CompilerParams(dimension_semantics=sem, vmem_limit_bytes=VMEM_LIMIT_BYTES)


def _resident(shape, index_map):
    return pl.BlockSpec(shape, index_map, pipeline_mode=pl.Buffered(1))


def _mod_spec(mod, tm, rows_per_group):
    g, r, d = mod.shape
    if r == 1:
        tiles_per_group = rows_per_group // tm
        return pl.BlockSpec((None, 1, d), lambda i: (i // tiles_per_group, 0, 0))
    assert g == 1 and r == tm
    return pl.BlockSpec((None, r, d), lambda i: (0, 0, 0))


def _ada_kernel(c_ref, w_ref, b_ref, o_ref):
    a = _silu(c_ref[...]).astype(BF16)
    o_ref[...] = _dot(a, w_ref[...].astype(BF16)) + b_ref[...]


def _ada(c_all, w_ada, b_ada):
    depth, d, n = w_ada.shape
    m = c_all.shape[0]
    tn = 1024
    return pl.pallas_call(
        _ada_kernel,
        grid=(depth, n // tn),
        in_specs=[
            pl.BlockSpec((m, d), lambda l, j: (0, 0)),
            pl.BlockSpec((None, d, tn), lambda l, j: (l, 0, j)),
            pl.BlockSpec((None, 1, tn), lambda l, j: (l, 0, j)),
        ],
        out_specs=pl.BlockSpec((None, m, tn), lambda l, j: (l, 0, j)),
        out_shape=jax.ShapeDtypeStruct((depth, m, n), F32),
        compiler_params=_params("arbitrary", "arbitrary"),
        name="ada_mod",
    )(c_all, w_ada, b_ada.reshape(depth, 1, n))


def _ffn_kernel(x_ref, sh_ref, sc_ref, gt_ref, gpre_ref, gpost_ref, wg_ref, wu_ref, wd_ref,
                o_ref, a_ref, *, tf):
    x = x_ref[...]
    h = (_rms(x, gpre_ref[...]) * (1.0 + sc_ref[...]) + sh_ref[...]).astype(BF16)
    f = wd_ref.shape[0]
    for c in range(f // tf):
        sl = slice(c * tf, (c + 1) * tf)
        g = _dot(h, wg_ref[:, sl])
        u = _dot(h, wu_ref[:, sl])
        a_ref[:, sl] = (_silu(g) * u).astype(BF16)
    y = _dot(a_ref[...], wd_ref[...])
    o_ref[...] = x + 0.5 * gt_ref[...] * _rms(y, gpost_ref[...])


def _ffn(x, sh, sc, gt, g_pre, g_post, w_in, w_down, *, tm, rows_per_group):
    n, d = x.shape
    f = w_down.shape[0]
    tf = 256
    assert f % tf == 0 and n % tm == 0
    row = pl.BlockSpec((tm, d), lambda i: (i, 0))
    vec = pl.BlockSpec((1, d), lambda i: (0, 0))
    return pl.pallas_call(
        functools.partial(_ffn_kernel, tf=tf),
        grid=(n // tm,),
        in_specs=[row, _mod_spec(sh, tm, rows_per_group), _mod_spec(sc, tm, rows_per_group),
                  _mod_spec(gt, tm, rows_per_group), vec, vec,
                  _resident((d, f), lambda i: (0, 0)), _resident((d, f), lambda i: (0, 1)),
                  _resident((f, d), lambda i: (0, 0))],
        out_specs=row,
        out_shape=jax.ShapeDtypeStruct((n, d), F32),
        scratch_shapes=[pltpu.VMEM((tm, f), BF16)],
        compiler_params=_params("arbitrary"),
        name="ffn",
    )(x, sh, sc, gt, g_pre, g_post, w_in, w_in, w_down)


def _rope(x, cos_t, sin_t):
    quarter = cos_t.shape[-1] // 8
    lane = lax.broadcasted_iota(jnp.int32, cos_t.shape, 1)
    outs = []
    for h in range(x.shape[-1] // LANES):
        xh = x[:, h * LANES:(h + 1) * LANES]
        partner = jnp.where(lane < quarter, pltpu.roll(xh, LANES - quarter, 1), pltpu.roll(xh, quarter, 1))
        outs.append(xh * cos_t + partner * sin_t)
    return jnp.concatenate(outs, axis=-1)


def _store_token_head_rows(ref, x):
    tm = x.shape[0]
    for h in range(N_HEADS):
        ref[pl.ds(h, tm, stride=N_HEADS), :] = x[:, h * LANES:(h + 1) * LANES]


def _mixin_kernel(x_ref, sh_ref, sc_ref, g_ref, cos_ref, sin_ref, w_ref, wa_ref, ba_ref,
                  gq_ref, gk_ref, gv_ref, la_ref,
                  mq_ref, mk_ref, mkb_ref, mv_ref, mvb_ref,
                  sq_ref, sk_ref, skb_ref, sv_ref, svb_ref, *, dk_total, dv_total, hd_total):
    x = x_ref[...]
    h = (_rms(x, g_ref[...]) * (1.0 + sc_ref[...]) + sh_ref[...]).astype(BF16)
    o = 0
    qk = _dot(h, w_ref[:, o:o + 2 * dk_total]); o += 2 * dk_total
    dk = dk_total // N_HEADS
    gq_ref[...] = qk[:, :dk_total] * (dk ** -0.5)
    gk_ref[...] = qk[:, dk_total:]
    gv_ref[...] = _dot(h, w_ref[:, o:o + dv_total]); o += dv_total
    glr = _dot(h, w_ref[:, o:o + LANES]); o += LANES
    alpha = _dot(glr.astype(BF16), wa_ref[...]) + ba_ref[...]
    la_ref[...] = (jnp.minimum(alpha, 0.0) - _softplus_neg_abs(alpha)) / GLA_TAU
    cos_t = cos_ref[...]
    sin_t = sin_ref[...]
    mq = _rope(_dot(h, w_ref[:, o:o + hd_total]), cos_t, sin_t); o += hd_total
    mq_ref[...] = mq.astype(BF16)
    mk = _rope(_dot(h, w_ref[:, o:o + hd_total]), cos_t, sin_t); o += hd_total
    _store_token_head_rows(mk_ref, mk)
    mkb_ref[...] = mk.astype(BF16)
    mv = _dot(h, w_ref[:, o:o + hd_total]); o += hd_total
    _store_token_head_rows(mv_ref, mv)
    if mvb_ref.ndim == 3:
        for c in range(mvb_ref.shape[0]):
            mvb_ref[c] = mv[c * MOBA_BLOCK:(c + 1) * MOBA_BLOCK, :].T.astype(BF16)
    else:
        mvb_ref[...] = mv.astype(BF16)
    sq_ref[...] = _dot(h, w_ref[:, o:o + hd_total]).astype(BF16); o += hd_total
    sk = _dot(h, w_ref[:, o:o + hd_total]); o += hd_total
    _store_token_head_rows(sk_ref, sk)
    skb_ref[...] = sk.astype(BF16)
    sv = _dot(h, w_ref[:, o:o + hd_total]); o += hd_total
    _store_token_head_rows(sv_ref, sv)
    svb_ref[...] = sv.astype(BF16)


def _mixin(x, sh, sc, g_pre, cos_t, sin_t, w1, w_alpha, b_alpha, *, tm, rows_per_group,
           dk_total, dv_total, hd_total, v_transposed):
    n, d = x.shape
    width = w1.shape[1]
    table_tiles = cos_t.shape[0] // tm
    row = lambda w: pl.BlockSpec((tm, w), lambda i: (i, 0))
    vec = lambda w: pl.BlockSpec((1, w), lambda i: (0, 0))
    table = pl.BlockSpec((tm, LANES), lambda i: (i % table_tiles, 0))
    hd = hd_total // N_HEADS
    outs = [(1, dk_total, F32), (1, dk_total, F32), (1, dv_total, F32), (1, dk_total, F32),
            (1, hd_total, BF16), (N_HEADS, hd, F32), (1, hd_total, BF16), (N_HEADS, hd, F32), (1, hd_total, BF16),
            (1, hd_total, BF16), (N_HEADS, hd, F32), (1, hd_total, BF16), (N_HEADS, hd, F32), (1, hd_total, BF16)]
    out_specs = [pl.BlockSpec((tm * r, w), lambda i: (i, 0)) for r, w, _ in outs]
    out_shape = [jax.ShapeDtypeStruct((n * r, w), t) for r, w, t in outs]
    if v_transposed:
        assert tm % MOBA_BLOCK == 0
        kb = tm // MOBA_BLOCK
        out_specs[8] = pl.BlockSpec((kb, hd_total, MOBA_BLOCK), lambda i: (i, 0, 0))
        out_shape[8] = jax.ShapeDtypeStruct((n // MOBA_BLOCK, hd_total, MOBA_BLOCK), BF16)
    return pl.pallas_call(
        functools.partial(_mixin_kernel, dk_total=dk_total, dv_total=dv_total, hd_total=hd_total),
        grid=(n // tm,),
        in_specs=[row(d), _mod_spec(sh, tm, rows_per_group), _mod_spec(sc, tm, rows_per_group),
                  vec(d), table, table,
                  _resident((d, width), lambda i: (0, 0)),
                  _resident(w_alpha.shape, lambda i: (0, 0)), vec(dk_total)],
        out_specs=out_specs,
        out_shape=out_shape,
        compiler_params=_params("arbitrary"),
        name="mixin",
    )(x, sh, sc, g_pre, cos_t, sin_t, w1, w_alpha, b_alpha)


def _cumsum_rows(x):
    n = x.shape[0]
    row = lax.broadcasted_iota(jnp.int32, x.shape, 0)
    s = 1
    while s < n:
        x = x + jnp.where(row >= s, pltpu.roll(x, s, 0), 0.0)
        s *= 2
    return x


def _gla_kernel(q_ref, k_ref, la_ref, v_ref, s0_ref, g_ref, o_ref, sout_ref, st_ref, *, chunk):
    j = pl.program_id(1)
    n_pairs = N_HEADS // 2
    dk = s0_ref.shape[1]

    @pl.when(j == 0)
    def _():
        for p in range(n_pairs):
            pair = jnp.concatenate([s0_ref[2 * p], s0_ref[2 * p + 1]], axis=0)
            st_ref[p] = pair.T

    tt = q_ref.shape[0]
    lane = lax.broadcasted_iota(jnp.int32, (1, LANES), 1)
    first = lane < dk
    tril = (lax.broadcasted_iota(jnp.int32, (chunk, chunk), 1)
            <= lax.broadcasted_iota(jnp.int32, (chunk, chunk), 0))
    g = g_ref[...]

    def body(c, carry):
        rs = pl.ds(pl.multiple_of(c * chunk, chunk), chunk)
        b = _cumsum_rows(la_ref[rs, :])
        b_last = b[chunk - 1:chunk, :]
        q = q_ref[rs, :]
        k = k_ref[rs, :]
        qe = q * jnp.exp(b)
        ki = k * jnp.exp(-b)
        kd = k * jnp.exp(b_last - b)
        eb = jnp.exp(b_last)
        for p in range(n_pairs):
            ls = slice(p * LANES, (p + 1) * LANES)
            st = st_ref[p]
            st_b = st.astype(BF16)
            ki_b = ki[:, ls].astype(BF16)
            upd = jnp.zeros_like(st)
            for hl in range(2):
                hh = 2 * p + hl
                msk = first if hl == 0 else jnp.logical_not(first)
                qm = jnp.where(msk, qe[:, ls], 0.0).astype(BF16)
                kdm = jnp.where(msk, kd[:, ls], 0.0).astype(BF16)
                vs = slice(hh * LANES, (hh + 1) * LANES)
                vh = v_ref[rs, vs].astype(BF16)
                att = jnp.where(tril, _dot_nt(qm, ki_b), 0.0)
                o = _dot_nt(qm, st_b) + _dot(att.astype(BF16), vh)
                o_ref[rs, vs] = _rms(o, g)
                upd = upd + _dot_tn(vh, kdm)
            st_ref[p] = st * eb[:, ls] + upd
        return carry

    lax.fori_loop(0, tt // chunk, body, 0)

    @pl.when(j == pl.num_programs(1) - 1)
    def _():
        for p in range(n_pairs):
            t = st_ref[p].T
            sout_ref[2 * p] = t[:dk]
            sout_ref[2 * p + 1] = t[dk:]


def _gla(q, k, la, v, s0, g_head, *, tt, chunk):
    b, t, dkt = q.shape
    dvt = v.shape[-1]
    _, nh, dk, dv = s0.shape
    assert nh == N_HEADS and 2 * dk == LANES and dv == LANES and t % tt == 0 and tt % chunk == 0
    seq = lambda w: pl.BlockSpec((None, tt, w), lambda i, j: (i, j, 0))
    st = pl.BlockSpec((None, nh, dk, dv), lambda i, j: (i, 0, 0, 0))
    return pl.pallas_call(
        functools.partial(_gla_kernel, chunk=chunk),
        grid=(b, t // tt),
        in_specs=[seq(dkt), seq(dkt), seq(dkt), seq(dvt), st, pl.BlockSpec((1, dv), lambda i, j: (0, 0))],
        out_specs=[seq(dvt), st],
        out_shape=[jax.ShapeDtypeStruct((b, t, dvt), F32), jax.ShapeDtypeStruct(s0.shape, F32)],
        scratch_shapes=[pltpu.VMEM((nh // 2, LANES, LANES), F32)],
        compiler_params=_params("arbitrary", "arbitrary"),
        name="gla",
    )(q, k, la, v, s0, g_head)


def _head_row_sums(x):
    sub = 8
    part = jnp.sum(x.reshape(x.shape[0] // sub, sub, x.shape[1]), axis=0)
    out = part[0:N_HEADS]
    for g in range(1, sub // N_HEADS):
        out = out + part[g * N_HEADS:(g + 1) * N_HEADS]
    return out


def _kmean_kernel(k_ref, o_ref):
    o_ref[...] = _head_row_sums(k_ref[...]) * (float(N_HEADS) / k_ref.shape[0])


def _kmean(k_rows, bsz):
    hd = k_rows.shape[1]
    t = k_rows.shape[0] // (bsz * N_HEADS)
    nb = t // MOBA_BLOCK
    rows = MOBA_BLOCK * N_HEADS
    out = pl.pallas_call(
        _kmean_kernel,
        grid=(bsz, nb),
        in_specs=[pl.BlockSpec((rows, hd), lambda i, j: (i * nb + j, 0))],
        out_specs=pl.BlockSpec((None, None, N_HEADS, hd), lambda i, j: (i, j, 0, 0)),
        out_shape=jax.ShapeDtypeStruct((bsz, nb, N_HEADS, hd), F32),
        compiler_params=_params("arbitrary", "arbitrary"),
        name="moba_kmean",
    )(k_rows)
    return out.transpose(0, 2, 1, 3)


def _top_blocks_bias(gate, allowed, blk, n_blocks):
    g = jnp.where(allowed, gate, NEG_INF)
    sel = jnp.zeros(gate.shape, jnp.bool_)
    for _ in range(MOBA_TOPK):
        m = jnp.max(g, axis=0, keepdims=True)
        is_m = jnp.logical_and(g == m, allowed)
        idx = jnp.min(jnp.where(is_m, blk, n_blocks), axis=0, keepdims=True)
        pick = jnp.logical_and(blk == idx, is_m)
        sel = jnp.logical_or(sel, pick)
        g = jnp.where(pick, NEG_INF, g)
    return jnp.where(sel, 0.0, NEG_INF)


def _moba_kernel(q_ref, k_ref, vt_ref, km_ref, o_ref, m_ref, l_ref, acc_ref, bias_ref, *, scale):
    own = pl.program_id(1)
    tq = q_ref.shape[0]
    nb = km_ref.shape[1]
    hd = LANES
    heads = [slice(h * hd, (h + 1) * hd) for h in range(N_HEADS)]
    blk = lax.broadcasted_iota(jnp.int32, (nb, tq), 0)
    allowed = blk < own
    causal = (lax.broadcasted_iota(jnp.int32, (tq, tq), 0)
              <= lax.broadcasted_iota(jnp.int32, (tq, tq), 1))
    own_rows = pl.ds(pl.multiple_of(own * tq, tq), tq)
    for h, hs in enumerate(heads):
        qh = q_ref[:, hs]
        km_hi, km_lo = _split_hi_lo(km_ref[h])
        gate = _dot_nt(km_hi, qh) + _dot_nt(km_lo, qh)
        bias_ref[h] = _top_blocks_bias(gate, allowed, blk, nb)
        s = jnp.where(causal, _dot_nt(k_ref[own_rows, hs], qh) * scale, NEG_INF)
        m = jnp.max(s, axis=0, keepdims=True)
        p = jnp.exp(s - m)
        m_ref[h] = m
        l_ref[h] = jnp.sum(p, axis=0, keepdims=True)
        acc_ref[h] = _dot(vt_ref[own, hs, :], p.astype(BF16))

    def blocks(n, width):
        rows = pl.ds(pl.multiple_of(n * tq, tq), width * tq)
        for h, hs in enumerate(heads):
            s_all = _dot_nt(k_ref[rows, hs], q_ref[:, hs]) * scale
            s = [s_all[i * tq:(i + 1) * tq] + bias_ref[h, pl.ds(n + i, 1), :] for i in range(width)]
            m = m_ref[h]
            m_new = m
            for si in s:
                m_new = jnp.maximum(m_new, jnp.max(si, axis=0, keepdims=True))
            a = jnp.exp(m - m_new)
            l = a * l_ref[h]
            acc = a * acc_ref[h]
            for i, si in enumerate(s):
                p = jnp.exp(si - m_new)
                l = l + jnp.sum(p, axis=0, keepdims=True)
                acc = acc + _dot(vt_ref[n + i, hs, :], p.astype(BF16))
            m_ref[h] = m_new
            l_ref[h] = l
            acc_ref[h] = acc

    def body(t, carry):
        blocks(2 * t, 2)
        return carry

    lax.fori_loop(0, own // 2, body, 0)

    @pl.when(own % 2 == 1)
    def _():
        blocks(own - 1, 1)
    for h, hs in enumerate(heads):
        o_ref[:, hs] = (acc_ref[h] / l_ref[h]).T.astype(o_ref.dtype)


def _moba(q, k, vt, kmean):
    b, t, w = q.shape
    nb = kmean.shape[2]
    tq = MOBA_BLOCK
    hd = w // N_HEADS
    assert hd == LANES and t % tq == 0
    full = pl.BlockSpec((None, t, w), lambda i, j: (i, 0, 0))
    tile = pl.BlockSpec((None, tq, w), lambda i, j: (i, j, 0))
    return pl.pallas_call(
        functools.partial(_moba_kernel, scale=hd ** -0.5),
        grid=(b, t // tq),
        in_specs=[tile, full, pl.BlockSpec((None, t // tq, w, tq), lambda i, j: (i, 0, 0, 0)),
                  pl.BlockSpec((None, N_HEADS, nb, hd), lambda i, j: (i, 0, 0, 0))],
        out_specs=tile,
        out_shape=jax.ShapeDtypeStruct((b, t, w), BF16),
        scratch_shapes=[pltpu.VMEM((N_HEADS, 1, tq), F32), pltpu.VMEM((N_HEADS, 1, tq), F32),
                        pltpu.VMEM((N_HEADS, hd, tq), F32), pltpu.VMEM((N_HEADS, nb, tq), F32)],
        compiler_params=_params("arbitrary", "arbitrary"),
        name="moba",
    )(q, k, vt, kmean)


SB_EXP_ZERO = -104.0


def _sb_heads_block(get_q, get_k, get_v, tri_ext, c_ref, acc_ref, scale, mask):
    tk = tri_ext.shape[0]
    c_max = None
    for h in range(N_HEADS):
        z = _dot_nt(get_q(h), get_k(h)) * scale
        sp = _softplus_neg_abs(z)
        log_beta = jnp.minimum(z, 0.0) - sp
        log_rem = -jnp.maximum(z, 0.0) - sp
        if mask is not None:
            log_rem = jnp.where(mask, log_rem, 0.0)
        hi, lo = _split_hi_lo(log_rem)
        res = _dot(hi, tri_ext) + _dot(lo, tri_ext)
        incl = res[:, :tk]
        c_old = c_ref[h]
        w = jnp.exp(log_beta + (incl - log_rem) + jnp.tile(c_old, (1, tk // LANES)))
        if mask is not None:
            w = jnp.where(mask, w, 0.0)
        acc_ref[h] += _dot(w.astype(BF16), get_v(h))
        c_new = c_old + res[:, tk:]
        c_ref[h] = c_new
        m = jnp.max(c_new)
        c_max = m if c_max is None else jnp.maximum(c_max, m)
    return c_max


def _sb_kernel(q_ref, k_ref, v_ref, o_ref, acc_ref, c_ref, flag_ref, *, scale):
    i = pl.program_id(1)
    tq = q_ref.shape[0]
    hd = LANES
    r = lax.broadcasted_iota(jnp.int32, (tq, tq), 0)
    c = lax.broadcasted_iota(jnp.int32, (tq, tq), 1)
    tri_ext = jnp.concatenate([(r >= c).astype(BF16), jnp.ones((tq, LANES), BF16)], axis=1)
    strict = c < r
    get_q = lambda h: q_ref[:, h * hd:(h + 1) * hd]

    def block(n, mask):
        rows = pl.ds(pl.multiple_of(n * tq, tq), tq)
        c_max = _sb_heads_block(get_q, lambda h: k_ref[rows, h * hd:(h + 1) * hd],
                                lambda h: v_ref[rows, h * hd:(h + 1) * hd],
                                tri_ext, c_ref, acc_ref, scale, mask)
        flag_ref[0] = (c_max > SB_EXP_ZERO).astype(jnp.int32)

    acc_ref[...] = jnp.zeros_like(acc_ref)
    c_ref[...] = jnp.zeros_like(c_ref)
    block(i, strict)

    def body(t, carry):
        @pl.when(flag_ref[0] > 0)
        def _():
            block(i - 1 - t, None)
        return carry

    lax.fori_loop(0, i, body, 0)
    for h in range(N_HEADS):
        o_ref[:, h * hd:(h + 1) * hd] = acc_ref[h].astype(o_ref.dtype)


def _sb(q, k, v):
    b, t, w = q.shape
    tq = min(256, t)
    hd = w // N_HEADS
    assert hd == LANES and t % tq == 0 and tq % LANES == 0
    full = pl.BlockSpec((None, t, w), lambda i, j: (i, 0, 0))
    tile = pl.BlockSpec((None, tq, w), lambda i, j: (i, j, 0))
    return pl.pallas_call(
        functools.partial(_sb_kernel, scale=hd ** -0.5),
        grid=(b, t // tq),
        in_specs=[tile, full, full],
        out_specs=tile,
        out_shape=jax.ShapeDtypeStruct((b, t, w), BF16),
        scratch_shapes=[pltpu.VMEM((N_HEADS, tq, hd), F32), pltpu.VMEM((N_HEADS, tq, LANES), F32),
                        pltpu.SMEM((1,), jnp.int32)],
        compiler_params=_params("arbitrary", "arbitrary"),
        name="stickbreak",
    )(q, k, v)


def _merge_kernel(x_ref, sh_ref, sc_ref, gt_ref, gpre_ref, gpost_ref, oa_ref, ob_ref, oc_ref,
                  w2_ref, wa_ref, wb_ref, wc_ref, wo_ref, o_ref, *, dv_total):
    x = x_ref[...]
    d = x.shape[-1]
    h = (_rms(x, gpre_ref[...]) * (1.0 + sc_ref[...]) + sh_ref[...]).astype(BF16)
    gr = _dot(h, w2_ref[:, :dv_total])
    oa = (oa_ref[...] * _silu(gr)).astype(BF16)
    merged = jax.nn.sigmoid(_dot(h, w2_ref[:, dv_total:dv_total + d])) * _dot(oa, wa_ref[...])
    merged = merged + (jax.nn.sigmoid(_dot(h, w2_ref[:, dv_total + d:dv_total + 2 * d]))
                       * _dot(ob_ref[...], wb_ref[...]))
    merged = merged + (jax.nn.sigmoid(_dot(h, w2_ref[:, dv_total + 2 * d:dv_total + 3 * d]))
                       * _dot(oc_ref[...], wc_ref[...]))
    m = _dot(merged.astype(BF16), wo_ref[...])
    o_ref[...] = x + gt_ref[...] * _rms(m, gpost_ref[...])


def _merge(x, sh, sc, gt, g_pre, g_post, oa, ob, oc, w2, wa, wb, wc, wo, *, tm, rows_per_group):
    n, d = x.shape
    dv_total = oa.shape[1]
    row = lambda w: pl.BlockSpec((tm, w), lambda i: (i, 0))
    vec = pl.BlockSpec((1, d), lambda i: (0, 0))
    res = lambda a: _resident(a.shape, lambda i: (0, 0))
    return pl.pallas_call(
        functools.partial(_merge_kernel, dv_total=dv_total),
        grid=(n // tm,),
        in_specs=[row(d), _mod_spec(sh, tm, rows_per_group), _mod_spec(sc, tm, rows_per_group),
                  _mod_spec(gt, tm, rows_per_group), vec, vec,
                  row(dv_total), row(ob.shape[1]), row(oc.shape[1]),
                  res(w2), res(wa), res(wb), res(wc), res(wo)],
        out_specs=row(d),
        out_shape=jax.ShapeDtypeStruct((n, d), F32),
        compiler_params=_params("arbitrary"),
        name="merge",
    )(x, sh, sc, gt, g_pre, g_post, oa, ob, oc, w2, wa, wb, wc, wo)


def _dec_gate_kernel(pt_ref, *refs, pages_per_step, pages_per_block, n_steps):
    page_refs = refs[:pages_per_step]
    q_ref = refs[pages_per_step]
    idx_ref = refs[pages_per_step + 1]
    km_ref = refs[pages_per_step + 2]
    j = pl.program_id(1)
    blocks_per_step = pages_per_step // pages_per_block
    inv = float(N_HEADS) / (pages_per_block * page_refs[0].shape[0])
    per_head = [[] for _ in range(N_HEADS)]
    for blk in range(blocks_per_step):
        s = None
        for r in range(pages_per_block):
            part = _head_row_sums(page_refs[blk * pages_per_block + r][...])
            s = part if s is None else s + part
        for h in range(N_HEADS):
            per_head[h].append(s[h:h + 1] * inv)
    rows = pl.ds(pl.multiple_of(j * blocks_per_step, blocks_per_step), blocks_per_step)
    for h in range(N_HEADS):
        km_ref[h, rows, :] = jnp.concatenate(per_head[h], axis=0)

    @pl.when(j == n_steps - 1)
    def _():
        nb = km_ref.shape[1]
        tq = q_ref.shape[0]
        blk = lax.broadcasted_iota(jnp.int32, (tq, nb), 1)
        lane = lax.broadcasted_iota(jnp.int32, (tq, LANES), 1)
        out = jnp.zeros((tq, LANES), jnp.int32)
        for h in range(N_HEADS):
            hs = slice(h * LANES, (h + 1) * LANES)
            q_hi, q_lo = _split_hi_lo(q_ref[:, hs])
            k_hi, k_lo = _split_hi_lo(km_ref[h])
            g = _dot_nt(q_hi, k_hi) + _dot_nt(q_hi, k_lo) + _dot_nt(q_lo, k_hi)
            for r in range(MOBA_TOPK):
                m = jnp.max(g, axis=-1, keepdims=True)
                idx = jnp.min(jnp.where(g == m, blk, nb), axis=-1, keepdims=True)
                out = jnp.where(lane == h * 4 + r, idx, out)
                g = jnp.where(blk == idx, NEG_INF, g)
        idx_ref[...] = out


def _dec_gate(cache_k, layer, page_table, q_pad):
    _, _, page_rows, hd = cache_k.shape
    page = page_rows // N_HEADS
    w = N_HEADS * hd
    db, n_pages = page_table.shape
    pages_per_block = MOBA_BLOCK // page
    blocks_per_step = 8
    pages_per_step = blocks_per_step * pages_per_block
    assert n_pages % pages_per_step == 0
    n_steps = n_pages // pages_per_step
    nb = n_pages // pages_per_block
    assert nb >= MOBA_TOPK

    def page_spec(r):
        return pl.BlockSpec((None, None, page_rows, hd),
                            lambda b, j, pt: (layer, pt[b, j * pages_per_step + r], 0, 0))

    grid_spec = pltpu.PrefetchScalarGridSpec(
        num_scalar_prefetch=1,
        grid=(db, n_steps),
        in_specs=[page_spec(r) for r in range(pages_per_step)]
        + [pl.BlockSpec((None, q_pad.shape[1], w), lambda b, j, pt: (b, 0, 0))],
        out_specs=pl.BlockSpec((None, q_pad.shape[1], LANES), lambda b, j, pt: (b, 0, 0)),
        scratch_shapes=[pltpu.VMEM((N_HEADS, nb, hd), F32)],
    )
    return pl.pallas_call(
        functools.partial(_dec_gate_kernel, pages_per_step=pages_per_step,
                          pages_per_block=pages_per_block, n_steps=n_steps),
        grid_spec=grid_spec,
        out_shape=jax.ShapeDtypeStruct((db, q_pad.shape[1], LANES), jnp.int32),
        compiler_params=_params("arbitrary", "arbitrary"),
        name="moba_dec_gate",
    )(page_table, *([cache_k] * pages_per_step), q_pad)


def _dec_moba_kernel(idx_ref, pt_ref, *refs, n_tok, pages_per_block, scale):
    per_head = MOBA_TOPK * pages_per_block
    n_sel = N_HEADS * per_head
    k_refs = refs[:n_sel]
    v_refs = refs[n_sel:2 * n_sel]
    q_ref, kn_ref, vn_ref, o_ref = refs[2 * n_sel:]
    t = pl.program_id(1)
    tq = q_ref.shape[0]
    page = kn_ref.shape[0]
    hd = LANES
    row = lax.broadcasted_iota(jnp.int32, (tq, page), 0)
    col = lax.broadcasted_iota(jnp.int32, (tq, page), 1)
    own_ok = jnp.logical_and(col <= row, col < n_tok)
    this_row = lax.broadcasted_iota(jnp.int32, (tq, hd), 0) == t

    @pl.when(t == 0)
    def _():
        o_ref[...] = jnp.zeros_like(o_ref)

    for h in range(N_HEADS):
        hs = slice(h * hd, (h + 1) * hd)
        q = q_ref[:, hs].astype(BF16)
        head_rows = pl.ds(h, page, stride=N_HEADS)
        scores = [jnp.where(own_ok, _dot_nt(q, kn_ref[:, hs].astype(BF16)) * scale, NEG_INF)]
        values = [vn_ref[:, hs].astype(BF16)]
        for i in range(h * per_head, (h + 1) * per_head):
            scores.append(_dot_nt(q, k_refs[i][head_rows, :].astype(BF16)) * scale)
            values.append(v_refs[i][head_rows, :].astype(BF16))
        m = scores[0].max(axis=-1, keepdims=True)
        for s in scores[1:]:
            m = jnp.maximum(m, s.max(axis=-1, keepdims=True))
        l = jnp.zeros((tq, 1), F32)
        acc = jnp.zeros((tq, hd), F32)
        for s, v in zip(scores, values):
            p = jnp.exp(s - m)
            l = l + jnp.sum(p, axis=-1, keepdims=True)
            acc = acc + _dot(p.astype(BF16), v)
        o_ref[:, hs] = jnp.where(this_row, acc / l, o_ref[:, hs])


def _dec_moba(cache_k, cache_v, layer, page_table, sel_flat, q_pad, k_new_pad, v_new_pad, *, n_tok):
    _, _, page_rows, hd = cache_k.shape
    nh = N_HEADS
    page = page_rows // nh
    db = page_table.shape[0]
    pages_per_block = MOBA_BLOCK // page
    tq = q_pad.shape[1]

    def page_spec(h, r, half):
        def index(b, t, sel, pt):
            n = sel[((b * n_tok + t) * nh + h) * MOBA_TOPK + r]
            return (layer, pt[b, n * pages_per_block + half], 0, 0)
        return pl.BlockSpec((None, None, page_rows, hd), index)

    sel_specs = [page_spec(h, r, half) for h in range(nh) for r in range(MOBA_TOPK)
                 for half in range(pages_per_block)]
    per_b = lambda rows: pl.BlockSpec((None, rows, nh * hd), lambda b, t, sel, pt: (b, 0, 0))
    grid_spec = pltpu.PrefetchScalarGridSpec(
        num_scalar_prefetch=2,
        grid=(db, n_tok),
        in_specs=sel_specs + sel_specs + [per_b(tq), per_b(page), per_b(page)],
        out_specs=per_b(tq),
    )
    n_sel = len(sel_specs)
    return pl.pallas_call(
        functools.partial(_dec_moba_kernel, n_tok=n_tok, pages_per_block=pages_per_block,
                          scale=hd ** -0.5),
        grid_spec=grid_spec,
        out_shape=jax.ShapeDtypeStruct((db, tq, nh * hd), F32),
        compiler_params=_params("arbitrary", "arbitrary"),
        name="moba_dec_attend",
    )(sel_flat, page_table, *([cache_k] * n_sel), *([cache_v] * n_sel), q_pad, k_new_pad, v_new_pad)


def _dec_sb_page(get_k, get_v, qblk_ref, tri, c_ref, acc_ref, flag_ref, scale, mask, n_cols):
    hd = LANES
    z = None
    for h in range(N_HEADS):
        zh = _dot(get_k(h).astype(BF16), qblk_ref[h * hd:(h + 1) * hd, :])
        z = zh if z is None else z + zh
    z = z * scale
    sp = _softplus_neg_abs(z)
    log_beta = jnp.minimum(z, 0.0) - sp
    log_rem = -jnp.maximum(z, 0.0) - sp
    if mask is not None:
        log_rem = jnp.where(mask, log_rem, 0.0)
    hi, lo = _split_hi_lo(log_rem)
    incl = _dot(tri, hi) + _dot(tri, lo)
    c = c_ref[0:1, :]
    w = jnp.exp(log_beta + (incl - log_rem) + c)
    if mask is not None:
        w = jnp.where(mask, w, 0.0)
    w = w.astype(BF16)
    for h in range(N_HEADS):
        acc_ref[h] += _dot_tn(w, get_v(h).astype(BF16))
    c_new = c + incl[0:1, :]
    c_ref[0:1, :] = c_new
    lane = lax.broadcasted_iota(jnp.int32, c_new.shape, 1)
    live = jnp.max(jnp.where(lane < n_cols, c_new, NEG_INF)) > SB_EXP_ZERO
    flag_ref[0] = live.astype(jnp.int32)


def _dec_sb_kernel(pt_ref, *refs, pages_per_step, n_steps, n_tok, scale):
    k_refs = refs[:pages_per_step]
    v_refs = refs[pages_per_step:2 * pages_per_step]
    qblk_ref, kn_ref, vn_ref, o_ref, c_ref, acc_ref, flag_ref = refs[2 * pages_per_step:]
    j = pl.program_id(1)
    page = kn_ref.shape[0]
    hd = LANES
    n_cols = N_HEADS * n_tok
    r = lax.broadcasted_iota(jnp.int32, (page, page), 0)
    cc = lax.broadcasted_iota(jnp.int32, (page, page), 1)
    tri = (cc >= r).astype(BF16)

    @pl.when(j == 0)
    def _():
        c_ref[...] = jnp.zeros_like(c_ref)
        acc_ref[...] = jnp.zeros_like(acc_ref)
        key = lax.broadcasted_iota(jnp.int32, (page, LANES), 0)
        lane = lax.broadcasted_iota(jnp.int32, (page, LANES), 1)
        mask = jnp.logical_and(key < lane % n_tok, lane < n_cols)
        _dec_sb_page(lambda h: kn_ref[:, h * hd:(h + 1) * hd], lambda h: vn_ref[:, h * hd:(h + 1) * hd],
                     qblk_ref, tri, c_ref, acc_ref, flag_ref, scale, mask, n_cols)

    for p in range(pages_per_step):
        @pl.when(flag_ref[0] > 0)
        def _():
            _dec_sb_page(lambda h: k_refs[p][pl.ds(h, page, stride=N_HEADS), :],
                         lambda h: v_refs[p][pl.ds(h, page, stride=N_HEADS), :],
                         qblk_ref, tri, c_ref, acc_ref, flag_ref, scale, None, n_cols)

    @pl.when(j == n_steps - 1)
    def _():
        for h in range(N_HEADS):
            o_ref[:, h * hd:(h + 1) * hd] = acc_ref[h][0:o_ref.shape[0], :]


def _dec_sb(cache_k, cache_v, layer, page_table, qblk, k_new_pad, v_new_pad, *, n_tok):
    _, _, page_rows, hd = cache_k.shape
    page = page_rows // N_HEADS
    w = N_HEADS * hd
    db, n_pages = page_table.shape
    pages_per_step = 8
    assert n_pages % pages_per_step == 0 and page == LANES
    n_steps = n_pages // pages_per_step
    rows_out = 16
    assert N_HEADS * n_tok <= rows_out

    def page_spec(r):
        return pl.BlockSpec((None, None, page_rows, hd),
                            lambda b, j, pt: (layer, pt[b, n_pages - 1 - (j * pages_per_step + r)], 0, 0))

    per_b = lambda rows, cols: pl.BlockSpec((None, rows, cols), lambda b, j, pt: (b, 0, 0))
    grid_spec = pltpu.PrefetchScalarGridSpec(
        num_scalar_prefetch=1,
        grid=(db, n_steps),
        in_specs=[page_spec(r) for r in range(pages_per_step)] * 2
        + [per_b(w, LANES), per_b(page, w), per_b(page, w)],
        out_specs=per_b(rows_out, w),
        scratch_shapes=[pltpu.VMEM((8, LANES), F32), pltpu.VMEM((N_HEADS, LANES, hd), F32),
                        pltpu.SMEM((1,), jnp.int32)],
    )
    return pl.pallas_call(
        functools.partial(_dec_sb_kernel, pages_per_step=pages_per_step, n_steps=n_steps,
                          n_tok=n_tok, scale=(w // N_HEADS) ** -0.5),
        grid_spec=grid_spec,
        out_shape=jax.ShapeDtypeStruct((db, rows_out, w), F32),
        compiler_params=_params("arbitrary", "arbitrary"),
        name="sb_dec",
    )(page_table, *([cache_k] * pages_per_step), *([cache_v] * pages_per_step),
      qblk, k_new_pad, v_new_pad)


def _rope_tables(pos, hd):
    rope_dim = hd // 4
    half = rope_dim // 2
    inv = ROPE_THETA ** (-jnp.arange(half, dtype=F32) / half)
    ang = pos.astype(F32)[:, None] * inv[None, :]
    cos, sin = jnp.cos(ang), jnp.sin(ang)
    n = pos.shape[0]
    ones = jnp.ones((n, hd - rope_dim), F32)
    cos_t = jnp.concatenate([cos, cos, ones], axis=-1)
    sin_t = jnp.concatenate([-sin, sin, jnp.zeros_like(ones)], axis=-1)
    return cos_t, sin_t


def kernel(x_prompt, x_sample, cache_moba_k, cache_moba_v, cache_sb_k, cache_sb_v, state_gla, page_table, c_prompt, c_sample, w_ada, b_ada, g_ffn1_pre, g_ffn1_post, g_mix_pre, g_mix_post, g_ffn2_pre, g_ffn2_post, w_ffn1_in, w_ffn1_down, w_ffn2_in, w_ffn2_down, w_in, w_gla_alpha, b_gla_alpha, g_gla_head, w_br_gla, w_br_moba, w_br_sb, w_out):
    bsz, t_p, d = x_prompt.shape
    db, t_s, _ = x_sample.shape
    depth = w_ada.shape[0]
    _, n_phys, page, nh, hd = cache_moba_k.shape
    _, _, _, dk, dv = state_gla.shape
    rank = w_gla_alpha.shape[1]
    assert nh == N_HEADS
    dk_total, dv_total, hd_total = nh * dk, nh * dv, nh * hd
    past_len = page_table.shape[1] * page
    n_p, n_s = bsz * t_p, db * t_s
    tm_p = min(512, t_p)
    t_pad = 8
    assert t_s <= t_pad and t_p % tm_p == 0

    mod = _ada(jnp.concatenate([c_prompt, c_sample], axis=0), w_ada, b_ada)

    o_gr = 2 * dk_total + dv_total
    o_glr = o_gr + dv_total
    o_m = o_glr + rank
    o_s = o_m + 3 * hd_total
    o_g = o_s + 3 * hd_total

    cos_p, sin_p = _rope_tables(jnp.arange(t_p, dtype=jnp.int32), hd)
    cos_s, sin_s = _rope_tables(past_len + jnp.arange(t_s, dtype=jnp.int32), hd)
    cos_s, sin_s = jnp.tile(cos_s, (db, 1)), jnp.tile(sin_s, (db, 1))

    yp = x_prompt.reshape(n_p, d)
    ys = x_sample.reshape(n_s, d)
    cache_mk4 = cache_moba_k.reshape(depth, n_phys, page * nh, hd)
    cache_mv4 = cache_moba_v.reshape(depth, n_phys, page * nh, hd)
    cache_sk4 = cache_sb_k.reshape(depth, n_phys, page * nh, hd)
    cache_sv4 = cache_sb_v.reshape(depth, n_phys, page * nh, hd)

    outs_p = [[] for _ in range(5)]
    outs_s = [[] for _ in range(5)]
    for l in range(depth):
        vec = lambda a: a[l].reshape(1, -1)
        w1 = jnp.concatenate(
            [w_in[l][:, :o_gr], jnp.pad(w_in[l][:, o_glr:o_m], ((0, 0), (0, LANES - rank))),
             w_in[l][:, o_m:o_g]], axis=1).astype(BF16)
        w2 = jnp.concatenate([w_in[l][:, o_gr:o_glr], w_in[l][:, o_g:]], axis=1).astype(BF16)
        w_alpha = jnp.pad(w_gla_alpha[l], ((0, LANES - rank), (0, 0))).astype(BF16)
        wf1_in, wf1_dn = w_ffn1_in[l].astype(BF16), w_ffn1_down[l].astype(BF16)
        wf2_in, wf2_dn = w_ffn2_in[l].astype(BF16), w_ffn2_down[l].astype(BF16)
        wa, wb, wc = w_br_gla[l].astype(BF16), w_br_moba[l].astype(BF16), w_br_sb[l].astype(BF16)
        wo = w_out[l].astype(BF16)
        mods = jnp.split(mod[l], N_MOD, axis=-1)
        mod_p = [m[:bsz].reshape(bsz, 1, d) for m in mods]
        mod_s = [jnp.repeat(m[bsz:], t_s, axis=0).reshape(1, n_s, d) for m in mods]

        kw = dict(tm=tm_p, rows_per_group=t_p)
        yp = _ffn(yp, mod_p[0], mod_p[1], mod_p[2], vec(g_ffn1_pre), vec(g_ffn1_post), wf1_in, wf1_dn, **kw)
        (gq, gk, gv, la, mq, mk, mkb, mv, mvb, sq, sk, skb, sv, svb) = _mixin(
            yp, mod_p[3], mod_p[4], vec(g_mix_pre), cos_p, sin_p, w1, w_alpha, vec(b_gla_alpha),
            dk_total=dk_total, dv_total=dv_total, hd_total=hd_total, v_transposed=True, **kw)
        seq = lambda a: a.reshape(bsz, t_p, a.shape[-1])
        o_a, st = _gla(seq(gq), seq(gk), seq(la), seq(gv), jnp.zeros((bsz, nh, dk, dv), F32),
                       vec(g_gla_head), tt=tm_p, chunk=math.gcd(t_p, GLA_CHUNK))
        o_b = _moba(seq(mq), seq(mkb), mvb.reshape(bsz, t_p // MOBA_BLOCK, hd_total, MOBA_BLOCK),
                    _kmean(mk, bsz))
        o_c = _sb(seq(sq), seq(skb), seq(svb))
        yp = _merge(yp, mod_p[3], mod_p[4], mod_p[5], vec(g_mix_pre), vec(g_mix_post),
                    o_a.reshape(n_p, dv_total), o_b.reshape(n_p, hd_total), o_c.reshape(n_p, hd_total),
                    w2, wa, wb, wc, wo, **kw)
        yp = _ffn(yp, mod_p[6], mod_p[7], mod_p[8], vec(g_ffn2_pre), vec(g_ffn2_post), wf2_in, wf2_dn, **kw)
        for lst, a in zip(outs_p, (mk, mv, sk, sv)):
            lst.append(a.reshape(bsz, t_p, nh, hd))
        outs_p[4].append(st)

        kw = dict(tm=n_s, rows_per_group=n_s)
        ys = _ffn(ys, mod_s[0], mod_s[1], mod_s[2], vec(g_ffn1_pre), vec(g_ffn1_post), wf1_in, wf1_dn, **kw)
        (gq, gk, gv, la, mq, mk, mkb, mv, mvb, sq, sk, skb, sv, svb) = _mixin(
            ys, mod_s[3], mod_s[4], vec(g_mix_pre), cos_s, sin_s, w1, w_alpha, vec(b_gla_alpha),
            dk_total=dk_total, dv_total=dv_total, hd_total=hd_total, v_transposed=False, **kw)
        seq = lambda a: a.reshape(db, t_s, -1)
        pad_t = lambda a, rows: jnp.pad(seq(a), ((0, 0), (0, rows - t_s), (0, 0)))
        o_a, st = _gla(pad_t(gq, t_pad), pad_t(gk, t_pad), pad_t(la, t_pad), pad_t(gv, t_pad),
                       state_gla[l], vec(g_gla_head), tt=t_pad, chunk=t_pad)
        o_a = o_a[:, :t_s]
        mq_pad = pad_t(mq.astype(F32), t_pad)
        sel = _dec_gate(cache_mk4, l, page_table, mq_pad)
        sel_flat = sel[:, :t_s, :nh * 4].reshape(db, t_s, nh, 4)[..., :MOBA_TOPK].reshape(-1)
        o_b = _dec_moba(cache_mk4, cache_mv4, l, page_table, sel_flat, mq_pad,
                        pad_t(mk, page), pad_t(mv, page), n_tok=t_s)[:, :t_s].astype(BF16)
        q4 = seq(sq).reshape(db, t_s, nh, hd)
        eye = jnp.eye(nh, dtype=BF16)
        qblk = jnp.einsum("bthd,hg->bhdgt", q4, eye).reshape(db, hd_total, nh * t_s)
        qblk = jnp.pad(qblk, ((0, 0), (0, 0), (0, LANES - nh * t_s)))
        o_raw = _dec_sb(cache_sk4, cache_sv4, l, page_table, qblk, pad_t(sk, page), pad_t(sv, page), n_tok=t_s)
        o_c = jnp.stack([o_raw[:, h * t_s:(h + 1) * t_s, h * hd:(h + 1) * hd] for h in range(nh)], axis=2)
        o_c = o_c.reshape(n_s, hd_total).astype(BF16)
        ys = _merge(ys, mod_s[3], mod_s[4], mod_s[5], vec(g_mix_pre), vec(g_mix_post),
                    o_a.reshape(n_s, dv_total), o_b.reshape(n_s, hd_total), o_c,
                    w2, wa, wb, wc, wo, **kw)
        ys = _ffn(ys, mod_s[6], mod_s[7], mod_s[8], vec(g_ffn2_pre), vec(g_ffn2_post), wf2_in, wf2_dn, **kw)
        for lst, a in zip(outs_s, (mk, mv, sk, sv)):
            lst.append(a.reshape(db, t_s, nh, hd))
        outs_s[4].append(st)

    stack = lambda lst: jnp.stack(lst)
    return (yp.reshape(bsz, t_p, d), ys.reshape(db, t_s, d),
            stack(outs_p[0]), stack(outs_p[1]), stack(outs_p[2]), stack(outs_p[3]), stack(outs_p[4]),
            stack(outs_s[0]), stack(outs_s[1]), stack(outs_s[2]), stack(outs_s[3]), stack(outs_s[4]))
```
